```python
import jax, jax.numpy as jnp
from jax import lax
import numpy as np

D_MODEL = 1024
BATCH = 16
SEQ = 4096
DEPTH = 4

N_EVEN = (DEPTH + 1) // 2
N_ODD = DEPTH // 2
EPS = 1e-6
OUT_SCALE = 0.5
CONV_DIM = D_MODEL // 2
CONV_WIDTH = 31
HGRN_DIM = D_MODEL // 2
HGRN_HEAD_DIM = 128
HGRN_HEADS = HGRN_DIM // HGRN_HEAD_DIM
HGRN_CHUNK = 64
SGU_DIM = D_MODEL // 2
SGU_GROUPS = 4
SGU_CHUNK = 128
ATTN_HEAD_DIM = 64
DIL_CONFIGS = ((128, 1), (512, 4), (2048, 16))
ATTN_HEADS_PER_CFG = 4
ATTN_HEADS = ATTN_HEADS_PER_CFG * len(DIL_CONFIGS)
ATTN_DIM = ATTN_HEADS * ATTN_HEAD_DIM
ATTN_BLOCK = 128
ROPE_THETA = 500000.0
ROT_DIM = ATTN_HEAD_DIM // 4
MEM_LEN = 256
XATTN_HEADS = 4
XATTN_HEAD_DIM = D_MODEL // XATTN_HEADS
D_FF = 4 * D_MODEL
EVEN_IN = 2 * CONV_DIM + 4 * HGRN_DIM
EVEN_MIX = CONV_DIM + HGRN_DIM
ODD_IN = 2 * SGU_DIM + 3 * ATTN_DIM
ODD_MIX = SGU_DIM + ATTN_DIM

kernel_name = "hybrid_conv_hgrn2_sgu_dilated_trunk"


def rmsnorm(x, g):
    x32 = x.astype(jnp.float32)
    y = x32 * lax.rsqrt(jnp.mean(x32 * x32, axis=-1, keepdims=True) + EPS)
    return (y * g.astype(jnp.float32)).astype(x.dtype)


def layernorm(x, g, b):
    x32 = x.astype(jnp.float32)
    mu = jnp.mean(x32, axis=-1, keepdims=True)
    xc = x32 - mu
    y = xc * lax.rsqrt(jnp.mean(xc * xc, axis=-1, keepdims=True) + EPS)
    return (y * g.astype(jnp.float32) + b.astype(jnp.float32)).astype(x.dtype)


def apply_partial_rope(x):
    T = x.shape[1]
    half = ROT_DIM // 2
    inv_freq = jnp.power(ROPE_THETA, -jnp.arange(half, dtype=jnp.float32) / half)
    ang = jnp.arange(T, dtype=jnp.float32)[:, None] * inv_freq[None, :]
    cos = jnp.cos(ang)[None, :, None, :]
    sin = jnp.sin(ang)[None, :, None, :]
    xr = x[..., :ROT_DIM].astype(jnp.float32)
    x1, x2 = xr[..., :half], xr[..., half:]
    rot = jnp.concatenate([x1 * cos - x2 * sin, x2 * cos + x1 * sin], axis=-1).astype(x.dtype)
    return jnp.concatenate([rot, x[..., ROT_DIM:]], axis=-1)


def conformer_conv(a_in, dw_w, dw_b, ln_g, ln_b):
    a, gate = jnp.split(a_in, 2, axis=-1)
    h = a * jax.nn.sigmoid(gate)
    h = lax.conv_general_dilated(h, dw_w[:, None, :], window_strides=(1,),
                                 padding=[(CONV_WIDTH - 1, 0)],
                                 dimension_numbers=('NWC', 'WIO', 'NWC'),
                                 feature_group_count=CONV_DIM) + dw_b
    h = layernorm(h, ln_g, ln_b)
    return jax.nn.silu(h)


def hgrn2(q, f_pre, i, g, lb, onorm_g):
    B, T, _ = q.shape
    H, K, C = HGRN_HEADS, HGRN_HEAD_DIM, HGRN_CHUNK
    n = T // C
    f = lb + (1.0 - lb) * jax.nn.sigmoid(f_pre.astype(jnp.float32))
    logf = jnp.log(f)
    k = 1.0 - f

    def chunks(a):
        return a.astype(jnp.float32).reshape(B, n, C, H, K).transpose(1, 0, 3, 2, 4)

    causal = jnp.tril(jnp.ones((C, C), dtype=bool))

    def step(S, inp):
        qc, kc, vc, gc = inp
        b = jnp.cumsum(gc, axis=2)
        o_inter = jnp.einsum('bhck,bhkv->bhcv', qc * jnp.exp(b), S)
        diff = b[:, :, :, None, :] - b[:, :, None, :, :]
        decay = jnp.exp(jnp.where(causal[:, :, None], diff, -jnp.inf))
        scores = jnp.einsum('bhtk,bhsk,bhtsk->bhts', qc, kc, decay)
        o_intra = jnp.einsum('bhts,bhsv->bhtv', scores, vc)
        b_last = b[:, :, -1:, :]
        S = jnp.exp(b_last[:, :, 0, :])[..., None] * S + jnp.einsum(
            'bhsk,bhsv->bhkv', kc * jnp.exp(b_last - b), vc)
        return S, o_inter + o_intra

    S0 = jnp.zeros((B, H, K, K), jnp.float32)
    _, o = lax.scan(step, S0, (chunks(q), chunks(k), chunks(i), chunks(logf)))
    o = o.transpose(1, 0, 3, 2, 4).reshape(B, T, H, K)
    o = rmsnorm(o, onorm_g).reshape(B, T, H * K)
    o = o * jax.nn.silu(g.astype(jnp.float32))
    return o.astype(q.dtype)


def chunked_sgu(z, ln_g, ln_b, w_s, b_s):
    z = jax.nn.gelu(z, approximate=False)
    u, v = jnp.split(z, 2, axis=-1)
    v = layernorm(v, ln_g, ln_b)
    B, T, _ = v.shape
    n = T // SGU_CHUNK
    dg = SGU_DIM // SGU_GROUPS
    vb = v.reshape(B, n, SGU_CHUNK, SGU_GROUPS, dg)
    mask = jnp.tril(jnp.ones((SGU_CHUNK, SGU_CHUNK), dtype=bool))
    w = jnp.where(mask[None], w_s, jnp.zeros_like(w_s))
    mixed = jnp.einsum('gts,bnsgc->bntgc', w, vb) + b_s.T[None, None, :, :, None]
    return u * mixed.reshape(B, T, SGU_DIM)


def dilated_window_attention(q, k, v, window, dilation):
    B, T, H, E = q.shape
    L = T // dilation
    span = window // dilation
    Lp = -(-L // ATTN_BLOCK) * ATTN_BLOCK
    nb = Lp // ATTN_BLOCK

    def to_blocks(a):
        a = a.reshape(B, L, dilation, H, E).transpose(0, 2, 3, 1, 4)
        a = jnp.pad(a, ((0, 0), (0, 0), (0, 0), (0, Lp - L), (0, 0)))
        return a.reshape(B, dilation, H, nb, ATTN_BLOCK, E)

    def with_prev(a):
        prev = jnp.pad(a, ((0, 0), (0, 0), (0, 0), (1, 0), (0, 0), (0, 0)))[:, :, :, :-1]
        return jnp.concatenate([prev, a], axis=4)

    qb = to_blocks(q)
    kc = with_prev(to_blocks(k))
    vc = with_prev(to_blocks(v))
    s = jnp.einsum('bdhnqe,bdhnke->bdhnqk', qb, kc,
                   preferred_element_type=jnp.float32) * (E ** -0.5)
    qi = jnp.arange(ATTN_BLOCK)[:, None]
    kj = jnp.arange(2 * ATTN_BLOCK)[None, :] - ATTN_BLOCK
    dist = qi - kj
    blk = jnp.arange(nb)[:, None, None]
    valid = (dist >= 0) & (dist <= span) & (blk * ATTN_BLOCK + kj >= 0)
    s = jnp.where(valid, s, -jnp.inf)
    m = jnp.max(s, axis=-1, keepdims=True)
    p = jnp.exp(s - m)
    denom = jnp.sum(p, axis=-1, keepdims=True)
    o = jnp.einsum('bdhnqk,bdhnke->bdhnqe', (p / denom).astype(v.dtype), vc)
    lse = (m + jnp.log(denom))[..., 0]
    o = o.reshape(B, dilation, H, Lp, E)[:, :, :, :L].transpose(0, 3, 1, 2, 4).reshape(B, T, H, E)
    lse = lse.reshape(B, dilation, H, Lp)[..., :L].transpose(0, 3, 1, 2).reshape(B, T, H)
    return o, lse


def even_mixer(h, w_in, dw_w, dw_b, ln_g, ln_b, lb, onorm_g, w_out):
    p = h @ w_in
    c0 = 2 * CONV_DIM
    a_in, q, f_pre, i, g = jnp.split(
        p, [c0, c0 + HGRN_DIM, c0 + 2 * HGRN_DIM, c0 + 3 * HGRN_DIM], axis=-1)
    a_out = conformer_conv(a_in, dw_w, dw_b, ln_g, ln_b)
    b_out = hgrn2(q, f_pre, i, g, lb, onorm_g)
    return jnp.concatenate([a_out, b_out], axis=-1) @ w_out


def odd_mixer(h, w_in, sgu_ln_g, sgu_ln_b, sgu_w, sgu_b, qn_g, kn_g, w_out):
    B, T, _ = h.shape
    p = h @ w_in
    c0 = 2 * SGU_DIM
    z, q, k, v = jnp.split(p, [c0, c0 + ATTN_DIM, c0 + 2 * ATTN_DIM], axis=-1)
    c_out = chunked_sgu(z, sgu_ln_g, sgu_ln_b, sgu_w, sgu_b)
    q = apply_partial_rope(rmsnorm(q.reshape(B, T, ATTN_HEADS, ATTN_HEAD_DIM), qn_g))
    k = apply_partial_rope(rmsnorm(k.reshape(B, T, ATTN_HEADS, ATTN_HEAD_DIM), kn_g))
    v = v.reshape(B, T, ATTN_HEADS, ATTN_HEAD_DIM)
    outs, lses = [], []
    for gi, (window, dilation) in enumerate(DIL_CONFIGS):
        sl = slice(gi * ATTN_HEADS_PER_CFG, (gi + 1) * ATTN_HEADS_PER_CFG)
        o, lse = dilated_window_attention(q[:, :, sl], k[:, :, sl], v[:, :, sl], window, dilation)
        outs.append(o)
        lses.append(lse)
    alpha = jax.nn.softmax(jnp.stack(lses, axis=0), axis=0)
    d_out = jnp.concatenate(
        [o * alpha[gi][..., None].astype(o.dtype) for gi, o in enumerate(outs)], axis=2)
    d_out = d_out.reshape(B, T, ATTN_DIM)
    return jnp.concatenate([c_out, d_out], axis=-1) @ w_out


def memory_cross_attention(h, mem_h, wq, wkv, qn_g, kn_g, wo):
    B, T, _ = h.shape
    M = mem_h.shape[1]
    q = rmsnorm((h @ wq).reshape(B, T, XATTN_HEADS, XATTN_HEAD_DIM), qn_g)
    k, v = jnp.split(mem_h @ wkv, 2, axis=-1)
    k = rmsnorm(k.reshape(B, M, XATTN_HEADS, XATTN_HEAD_DIM), kn_g)
    v = v.reshape(B, M, XATTN_HEADS, XATTN_HEAD_DIM)
    s = jnp.einsum('bthe,bmhe->bhtm', q, k,
                   preferred_element_type=jnp.float32) * (XATTN_HEAD_DIM ** -0.5)
    pr = jax.nn.softmax(s, axis=-1).astype(v.dtype)
    o = jnp.einsum('bhtm,bmhe->bthe', pr, v).reshape(B, T, XATTN_HEADS * XATTN_HEAD_DIM)
    return o @ wo


def squared_relu_mlp(h, w1, w2):
    return jnp.square(jax.nn.relu(h @ w1)) @ w2


def setup_inputs(seed: int = 0) -> dict:
    key = jax.random.key(seed)
    kit = iter(list(jax.random.split(key, 32)))

    def nrm(shape, scale):
        return scale * jax.random.normal(next(kit), shape, jnp.float32)

    def gain(shape):
        return 1.0 + nrm(shape, 0.02)

    return {
        'x': nrm((BATCH, SEQ, D_MODEL), 1.0),
        'mem': nrm((BATCH, MEM_LEN, D_MODEL), 1.0),
        'norm_mix_g': gain((DEPTH, D_MODEL)),
        'ev_w_in': nrm((N_EVEN, D_MODEL, EVEN_IN), D_MODEL ** -0.5),
        'conv_dw_w': nrm((N_EVEN, CONV_WIDTH, CONV_DIM), CONV_WIDTH ** -0.5),
        'conv_dw_b': nrm((N_EVEN, CONV_DIM), 0.02),
        'conv_ln_g': gain((N_EVEN, CONV_DIM)),
        'conv_ln_b': nrm((N_EVEN, CONV_DIM), 0.02),
        'hgrn_lb_logits': nrm((N_EVEN, HGRN_DIM), 0.5),
        'hgrn_onorm_g': gain((N_EVEN, HGRN_HEAD_DIM)),
        'ev_w_out': nrm((N_EVEN, EVEN_MIX, D_MODEL), OUT_SCALE * EVEN_MIX ** -0.5),
        'od_w_in': nrm((N_ODD, D_MODEL, ODD_IN), D_MODEL ** -0.5),
        'sgu_ln_g': gain((N_ODD, SGU_DIM)),
        'sgu_ln_b': nrm((N_ODD, SGU_DIM), 0.02),
        'sgu_w': nrm((N_ODD, SGU_GROUPS, SGU_CHUNK, SGU_CHUNK), SGU_CHUNK ** -0.5),
        'sgu_b': gain((N_ODD, SGU_GROUPS, SGU_CHUNK)),
        'attn_qnorm_g': gain((N_ODD, ATTN_HEAD_DIM)),
        'attn_knorm_g': gain((N_ODD, ATTN_HEAD_DIM)),
        'od_w_out': nrm((N_ODD, ODD_MIX, D_MODEL), OUT_SCALE * ODD_MIX ** -0.5),
        'norm_xattn_g': gain((DEPTH, D_MODEL)),
        'norm_mem_g': gain((DEPTH, D_MODEL)),
        'xattn_wq': nrm((DEPTH, D_MODEL, XATTN_HEADS * XATTN_HEAD_DIM), D_MODEL ** -0.5),
        'xattn_wkv': nrm((DEPTH, D_MODEL, 2 * XATTN_HEADS * XATTN_HEAD_DIM), D_MODEL ** -0.5),
        'xattn_qnorm_g': gain((DEPTH, XATTN_HEAD_DIM)),
        'xattn_knorm_g': gain((DEPTH, XATTN_HEAD_DIM)),
        'xattn_wo': nrm((DEPTH, XATTN_HEADS * XATTN_HEAD_DIM, D_MODEL),
                        OUT_SCALE * (XATTN_HEADS * XATTN_HEAD_DIM) ** -0.5),
        'norm_mlp_g': gain((DEPTH, D_MODEL)),
        'mlp_w1': nrm((DEPTH, D_MODEL, D_FF), D_MODEL ** -0.5),
        'mlp_w2': nrm((DEPTH, D_FF, D_MODEL), OUT_SCALE * D_FF ** -0.5),
    }


def reference(x, mem, norm_mix_g, ev_w_in, conv_dw_w, conv_dw_b, conv_ln_g, conv_ln_b,
              hgrn_lb_logits, hgrn_onorm_g, ev_w_out, od_w_in, sgu_ln_g, sgu_ln_b, sgu_w,
              sgu_b, attn_qnorm_g, attn_knorm_g, od_w_out, norm_xattn_g, norm_mem_g,
              xattn_wq, xattn_wkv, xattn_qnorm_g, xattn_knorm_g, xattn_wo, norm_mlp_g,
              mlp_w1, mlp_w2):
    lb_all = jnp.cumsum(jax.nn.softmax(hgrn_lb_logits.astype(jnp.float32), axis=0), axis=0)
    lb_all = lb_all - lb_all[0]
    for l in range(DEPTH):
        h = rmsnorm(x, norm_mix_g[l])
        if l % 2 == 0:
            e = l // 2
            x = x + even_mixer(h, ev_w_in[e], conv_dw_w[e], conv_dw_b[e], conv_ln_g[e],
                               conv_ln_b[e], lb_all[e], hgrn_onorm_g[e], ev_w_out[e])
        else:
            o = l // 2
            x = x + odd_mixer(h, od_w_in[o], sgu_ln_g[o], sgu_ln_b[o], sgu_w[o], sgu_b[o],
                              attn_qnorm_g[o], attn_knorm_g[o], od_w_out[o])
        h = rmsnorm(x, norm_xattn_g[l])
        mem_h = rmsnorm(mem, norm_mem_g[l])
        x = x + memory_cross_attention(h, mem_h, xattn_wq[l], xattn_wkv[l], xattn_qnorm_g[l],
                                       xattn_knorm_g[l], xattn_wo[l])
        h = rmsnorm(x, norm_mlp_g[l])
        x = x + squared_relu_mlp(h, mlp_w1[l], mlp_w2[l])
    return x
```

```python
import functools

import numpy as np
import jax
import jax.numpy as jnp
from jax import lax
from jax.experimental import pallas as pl
from jax.experimental.pallas import tpu as pltpu

F32 = jnp.float32
BF16 = jnp.bfloat16

D_MODEL = 1024
DEPTH = 4
EPS = 1e-6
CONV_DIM = 512
CONV_WIDTH = 31
HGRN_DIM = 512
HGRN_HEAD_DIM = 128
HGRN_HEADS = 4
HGRN_CHUNK = 64
SGU_DIM = 512
SGU_GROUPS = 4
SGU_CHUNK = 128
ATTN_HEAD_DIM = 64
DIL_CONFIGS = ((128, 1), (512, 4), (2048, 16))
ATTN_HEADS_PER_CFG = 4
ATTN_CFG_DIM = ATTN_HEADS_PER_CFG * ATTN_HEAD_DIM
ATTN_DIM = ATTN_CFG_DIM * len(DIL_CONFIGS)
ATTN_BLOCK = 128
ATTN_SPAN = 128
ROPE_THETA = 500000.0
ROT_DIM = 16
MEM_LEN = 256
XATTN_HEADS = 4
XATTN_HEAD_DIM = 256
D_FF = 4096

LANES = 128
CONV_HALO = 32
NEG_BIG = -1e30
VMEM_LIMIT = 56 * 1024 * 1024


def _cparams(sem):
    return pltpu.CompilerParams(dimension_semantics=sem, vmem_limit_bytes=VMEM_LIMIT)


def _const_spec(shape):
    nd = len(shape)
    return pl.BlockSpec(shape, lambda *_: (0,) * nd, pipeline_mode=pl.Buffered(1))


def _dot(a, b):
    return jnp.dot(a, b, preferred_element_type=F32)


def _dot_nt(a, b):
    return lax.dot_general(a, b, (((1,), (1,)), ((), ())), preferred_element_type=F32)


def _dot_tn(a, b):
    return lax.dot_general(a, b, (((0,), (0,)), ((), ())), preferred_element_type=F32)


def _rms(x, g):
    return x * lax.rsqrt(jnp.mean(x * x, axis=-1, keepdims=True) + EPS) * g


def _layernorm(x, g, b):
    mu = jnp.mean(x, axis=-1, keepdims=True)
    xc = x - mu
    return xc * lax.rsqrt(jnp.mean(xc * xc, axis=-1, keepdims=True) + EPS) * g + b


def _sigmoid(x):
    return 1.0 / (1.0 + jnp.exp(-x))


def _norm_proj_body(x_ref, g_ref, w_ref, *o_refs, splits):
    h = _rms(x_ref[...], g_ref[...]).astype(BF16)
    off = 0
    for o_ref, n in zip(o_refs, splits):
        o_ref[...] = _dot(h, w_ref[:, off:off + n]).astype(o_ref.dtype)
        off += n


def _norm_proj(x2d, g, w, splits, dtypes, tm):
    m, d = x2d.shape
    n_total = w.shape[1]
    return pl.pallas_call(
        functools.partial(_norm_proj_body, splits=splits),
        grid=(m // tm,),
        in_specs=[pl.BlockSpec((tm, d), lambda i: (i, 0)),
                  _const_spec((1, d)), _const_spec((d, n_total))],
        out_specs=[pl.BlockSpec((tm, n), lambda i: (i, 0)) for n in splits],
        out_shape=[jax.ShapeDtypeStruct((m, n), dt) for n, dt in zip(splits, dtypes)],
        compiler_params=_cparams(("parallel",)),
        name="norm_proj",
    )(x2d, g.reshape(1, d), w)


def _conv_body(a_ref, w_ref, b_ref, lg_ref, lb_ref, o_ref, hbuf, *, tt, rs):
    c = CONV_DIM

    @pl.when(pl.program_id(1) == 0)
    def _():
        hbuf[0:CONV_HALO, :] = jnp.zeros((CONV_HALO, c), F32)

    hbuf[CONV_HALO:CONV_HALO + tt, :] = a_ref[:, :c] * _sigmoid(a_ref[:, c:])
    base = CONV_HALO - (CONV_WIDTH - 1)
    for r0 in range(0, tt, rs):
        acc = jnp.zeros((rs, c), F32) + b_ref[...]
        for j in range(CONV_WIDTH):
            acc = acc + w_ref[j:j + 1, :] * hbuf[base + j + r0:base + j + r0 + rs, :]
        y = _layernorm(acc, lg_ref[...], lb_ref[...])
        o_ref[r0:r0 + rs, :] = (y * _sigmoid(y)).astype(o_ref.dtype)
    hbuf[0:CONV_HALO, :] = hbuf[tt:tt + CONV_HALO, :]


def _conformer_conv(a_in, dw_w, dw_b, ln_g, ln_b, batch, seq, tt=512, rs=64):
    c = CONV_DIM
    a3 = a_in.reshape(batch, seq, 2 * c)
    w_pad = jnp.zeros((CONV_HALO, c), F32).at[:CONV_WIDTH].set(dw_w)
    out = pl.pallas_call(
        functools.partial(_conv_body, tt=tt, rs=rs),
        grid=(batch, seq // tt),
        in_specs=[pl.BlockSpec((None, tt, 2 * c), lambda b, t: (b, t, 0)),
                  _const_spec((CONV_HALO, c)), _const_spec((1, c)),
                  _const_spec((1, c)), _const_spec((1, c))],
        out_specs=pl.BlockSpec((None, tt, c), lambda b, t: (b, t, 0)),
        out_shape=jax.ShapeDtypeStruct((batch, seq, c), BF16),
        scratch_shapes=[pltpu.VMEM((CONV_HALO + tt, c), F32)],
        compiler_params=_cparams(("parallel", "arbitrary")),
        name="conformer_conv",
    )(a3, w_pad, dw_b.reshape(1, c), ln_g.reshape(1, c), ln_b.reshape(1, c))
    return out.reshape(batch * seq, c)


_HGRN_LEVELS = (32, 16, 8, 4, 2, 1)


def _hgrn_tables():
    n = HGRN_CHUNK
    t = np.arange(n)[:, None]
    u = np.arange(n)[None, :]
    blocks = [u <= t, u > t]
    masks = []
    for m in _HGRN_LEVELS:
        c = (t // (2 * m)) * (2 * m)
        upper = (t - c) >= m
        blocks.append(upper & (u >= c + m) & (u <= t))
        blocks.append((~upper) & (u >= t + 1) & (u <= c + m - 1))
        cs = (u // (2 * m)) * (2 * m)
        masks.append(((t // (2 * m)) == (u // (2 * m))) & upper & ((u - cs) < m))
    d = np.concatenate(blocks, axis=0).astype(np.float32)
    return np.concatenate([d, d, d], axis=1), np.stack(masks).astype(np.float32)


def _hgrn_body(q_ref, f_ref, i_ref, g_ref, lb_ref, og_ref, d_ref, m_ref, o_ref, st_ref,
               *, n_chunks):
    n = HGRN_CHUNK
    hd = HGRN_HEAD_DIM

    @pl.when(pl.program_id(1) == 0)
    def _():
        st_ref[...] = jnp.zeros(st_ref.shape, F32)

    lb = lb_ref[...]

    def chunk(ci, carry):
        rows = pl.ds(pl.multiple_of(ci * n, n), n)
        f = lb + (1.0 - lb) * _sigmoid(f_ref[rows, :])
        logf = jnp.log(f)
        kk = 1.0 - f
        hi = logf.astype(BF16)
        r1 = logf - hi.astype(F32)
        mid = r1.astype(BF16)
        lo = (r1 - mid.astype(F32)).astype(BF16)
        expo = _dot(d_ref[...], jnp.concatenate([hi, mid, lo], axis=0))
        q = q_ref[rows, :]
        v = i_ref[rows, :]
        g = g_ref[rows, :]
        for h in range(HGRN_HEADS):
            ls = slice(h * hd, (h + 1) * hd)
            qh, kh, vh, gh = q[:, ls], kk[:, ls], v[:, ls], g[:, ls]
            eh = expo[:, ls]
            e_b = eh[0:n]
            e_bl = eh[n:2 * n]
            st = st_ref[h]
            vb = vh.astype(BF16)
            o = _dot_nt((qh * jnp.exp(e_b)).astype(BF16), st.astype(BF16))
            sc = jnp.zeros((n, n), F32)
            for li in range(len(_HGRN_LEVELS)):
                r0 = (2 + 2 * li) * n
                qs = (qh * jnp.exp(eh[r0:r0 + n])).astype(BF16)
                ks = (kh * jnp.exp(eh[r0 + n:r0 + 2 * n])).astype(BF16)
                sc = sc + m_ref[li] * _dot_nt(qs, ks)
            o = o + _dot(sc.astype(BF16), vb)
            o = o + jnp.sum(qh * kh, axis=-1, keepdims=True) * vh
            dec = jnp.exp(e_b[n - 1:n, :])
            st_ref[h] = st * dec + _dot_tn(vb, (kh * jnp.exp(e_bl)).astype(BF16))
            on = _rms(o, og_ref[...])
            o_ref[rows, ls] = (on * (gh * _sigmoid(gh))).astype(o_ref.dtype)
        return carry

    lax.fori_loop(0, n_chunks, chunk, 0)


def _hgrn2(q, f_pre, i, g, lb, onorm_g, batch, seq, tt=512):
    c = HGRN_DIM
    d3, masks = _hgrn_tables()
    spec = pl.BlockSpec((None, tt, c), lambda b, t: (b, t, 0))
    out = pl.pallas_call(
        functools.partial(_hgrn_body, n_chunks=tt // HGRN_CHUNK),
        grid=(batch, seq // tt),
        in_specs=[spec, spec, spec, spec,
                  _const_spec((1, c)), _const_spec((1, HGRN_HEAD_DIM)),
                  _const_spec(d3.shape), _const_spec(masks.shape)],
        out_specs=spec,
        out_shape=jax.ShapeDtypeStruct((batch, seq, c), BF16),
        scratch_shapes=[pltpu.VMEM((HGRN_HEADS, HGRN_HEAD_DIM, HGRN_HEAD_DIM), F32)],
        compiler_params=_cparams(("parallel", "arbitrary")),
        name="hgrn2",
    )(q.reshape(batch, seq, c), f_pre.reshape(batch, seq, c), i.reshape(batch, seq, c),
      g.reshape(batch, seq, c), lb.reshape(1, c), onorm_g.reshape(1, HGRN_HEAD_DIM),
      jnp.asarray(d3, BF16), jnp.asarray(masks, F32))
    return out.reshape(batch * seq, c)


def _sgu_body(z_ref, lg_ref, lb_ref, w_ref, bias_ref, o_ref, *, tt):
    c = SGU_DIM
    gw = c // SGU_GROUPS
    z = z_ref[...]
    z = 0.5 * z * (1.0 + lax.erf(z * np.float32(1.0 / np.sqrt(2.0))))
    u = z[:, :c]
    v = _layernorm(z[:, c:], lg_ref[...], lb_ref[...]).astype(BF16)
    for ci in range(tt // SGU_CHUNK):
        rs = slice(ci * SGU_CHUNK, (ci + 1) * SGU_CHUNK)
        for gi in range(SGU_GROUPS):
            ls = slice(gi * gw, (gi + 1) * gw)
            mixed = _dot(w_ref[gi], v[rs, ls]) + bias_ref[:, ls]
            o_ref[rs, ls] = (u[rs, ls] * mixed).astype(o_ref.dtype)


def _chunked_sgu(z, ln_g, ln_b, w_s, b_s, tt=512):
    m = z.shape[0]
    c = SGU_DIM
    tril = np.tril(np.ones((SGU_CHUNK, SGU_CHUNK), dtype=bool))
    w = jnp.where(tril[None], w_s, 0.0).astype(BF16)
    bias = jnp.repeat(b_s.T, c // SGU_GROUPS, axis=1)
    return pl.pallas_call(
        functools.partial(_sgu_body, tt=tt),
        grid=(m // tt,),
        in_specs=[pl.BlockSpec((tt, 2 * c), lambda i: (i, 0)),
                  _const_spec((1, c)), _const_spec((1, c)),
                  _const_spec(w.shape), _const_spec((SGU_CHUNK, c))],
        out_specs=pl.BlockSpec((tt, c), lambda i: (i, 0)),
        out_shape=jax.ShapeDtypeStruct((m, c), BF16),
        compiler_params=_cparams(("parallel",)),
        name="chunked_sgu",
    )(z, ln_g.reshape(1, c), ln_b.reshape(1, c), w, bias)


def _rope_tables(seq):
    half = ROT_DIM // 2
    inv_freq = jnp.power(ROPE_THETA, -jnp.arange(half, dtype=F32) / half)
    ang = jnp.arange(seq, dtype=F32)[:, None] * inv_freq[None, :]
    cos, sin = jnp.cos(ang), jnp.sin(ang)
    ones = jnp.ones((seq, ATTN_HEAD_DIM - ROT_DIM), F32)
    zeros = jnp.zeros((seq, ATTN_HEAD_DIM - ROT_DIM), F32)
    zh = jnp.zeros((seq, half), F32)
    c_head = jnp.concatenate([cos, cos, ones], axis=1)
    s_dn_head = jnp.concatenate([zh, sin, zeros], axis=1)
    s_up_head = jnp.concatenate([-sin, zh, zeros], axis=1)
    tile = lambda a: jnp.concatenate([a, a], axis=1)
    return tile(c_head), tile(s_dn_head), tile(s_up_head)


def _qk_prep_body(q_ref, k_ref, qg_ref, kg_ref, c_ref, sd_ref, su_ref, bd_ref, qo_ref, ko_ref):
    half = ROT_DIM // 2
    for x_ref, g_ref, o_ref in ((q_ref, qg_ref, qo_ref), (k_ref, kg_ref, ko_ref)):
        x = x_ref[...]
        ms = _dot((x * x).astype(BF16), bd_ref[...])
        y = x * lax.rsqrt(ms + EPS) * g_ref[...]
        for j in range(ATTN_DIM // LANES):
            yj = y[:, j * LANES:(j + 1) * LANES]
            out = (yj * c_ref[...] + pltpu.roll(yj, half, 1) * sd_ref[...]
                   + pltpu.roll(yj, LANES - half, 1) * su_ref[...])
            o_ref[:, j * LANES:(j + 1) * LANES] = out.astype(o_ref.dtype)


def _qk_prep(q, k, qn_g, kn_g, seq, tm=512):
    m = q.shape[0]
    c_t, sd_t, su_t = _rope_tables(seq)
    head = np.arange(ATTN_DIM) // ATTN_HEAD_DIM
    bd = jnp.asarray((head[:, None] == head[None, :]).astype(np.float32) / ATTN_HEAD_DIM, BF16)
    n_t = seq // tm
    row = pl.BlockSpec((tm, ATTN_DIM), lambda i: (i, 0))
    tab = pl.BlockSpec((tm, LANES), lambda i: (i % n_t, 0))
    tile_g = lambda g: jnp.tile(g, ATTN_DIM // ATTN_HEAD_DIM).reshape(1, ATTN_DIM)
    return pl.pallas_call(
        _qk_prep_body,
        grid=(m // tm,),
        in_specs=[row, row, _const_spec((1, ATTN_DIM)), _const_spec((1, ATTN_DIM)),
                  tab, tab, tab, _const_spec((ATTN_DIM, ATTN_DIM))],
        out_specs=[row, row],
        out_shape=[jax.ShapeDtypeStruct((m, ATTN_DIM), BF16)] * 2,
        compiler_params=_cparams(("parallel",)),
        name="qk_prep",
    )(q, k, tile_g(qn_g), tile_g(kn_g), c_t, sd_t, su_t, bd)


def _dil_attn_body(q_ref, kc_ref, kp_ref, vc_ref, vp_ref, o_ref, l_ref):
    blk = ATTN_BLOCK
    first = pl.program_id(2) == 0
    qi = lax.broadcasted_iota(jnp.int32, (blk, 2 * blk), 0)
    cj = lax.broadcasted_iota(jnp.int32, (blk, 2 * blk), 1)
    dist = qi - (cj - blk)
    valid = (dist >= 0) & (dist <= ATTN_SPAN) & ((cj >= blk) | jnp.logical_not(first))
    lane = lax.broadcasted_iota(jnp.int32, (1, LANES), 1)
    q = q_ref[...]
    k = jnp.concatenate([kp_ref[...], kc_ref[...]], axis=0)
    v = jnp.concatenate([vp_ref[...], vc_ref[...]], axis=0)
    scale = np.float32(ATTN_HEAD_DIM ** -0.5)
    for pair in range(ATTN_CFG_DIM // LANES):
        ls = slice(pair * LANES, (pair + 1) * LANES)
        qp, kp, vp = q[:, ls], k[:, ls], v[:, ls]
        o_pair = None
        l_pair = None
        for hh in range(LANES // ATTN_HEAD_DIM):
            in_head = (lane // ATTN_HEAD_DIM) == hh
            s = _dot_nt(jnp.where(in_head, qp, jnp.zeros_like(qp)), kp) * scale
            s = jnp.where(valid, s, NEG_BIG)
            mx = jnp.max(s, axis=-1, keepdims=True)
            p = jnp.exp(s - mx)
            den = jnp.sum(p, axis=-1, keepdims=True)
            o_h = _dot((p / den).astype(BF16), vp)
            lse = mx + jnp.log(den)
            o_pair = o_h if o_pair is None else jnp.where(in_head, o_h, o_pair)
            l_pair = (jnp.broadcast_to(lse, (blk, LANES)) if l_pair is None
                      else jnp.where(in_head, lse, l_pair))
        o_ref[:, ls] = o_pair
        l_ref[:, ls] = l_pair


def _dilated_attention(qn, kn, v, cfg, dilation, batch, seq):
    ln = seq // dilation
    nb = ln // ATTN_BLOCK
    groups = ATTN_DIM // ATTN_CFG_DIM
    view = lambda a: a.reshape(batch, ln, dilation * ATTN_DIM)
    cur = pl.BlockSpec((None, ATTN_BLOCK, ATTN_CFG_DIM),
                       lambda b, r, n: (b, n, r * groups + cfg))
    prev = pl.BlockSpec((None, ATTN_BLOCK, ATTN_CFG_DIM),
                        lambda b, r, n: (b, jnp.maximum(n - 1, 0), r * groups + cfg))
    out = pl.BlockSpec((None, ATTN_BLOCK, ATTN_CFG_DIM), lambda b, r, n: (b, n, r))
    o, lse = pl.pallas_call(
        _dil_attn_body,
        grid=(batch, dilation, nb),
        in_specs=[cur, cur, prev, cur, prev],
        out_specs=[out, out],
        out_shape=[jax.ShapeDtypeStruct((batch, ln, dilation * ATTN_CFG_DIM), F32)] * 2,
        compiler_params=_cparams(("parallel", "parallel", "arbitrary")),
        name=f"dil_attn_d{dilation}",
    )(view(qn), view(kn), view(kn), view(v), view(v))
    return o.reshape(batch * seq, ATTN_CFG_DIM), lse.reshape(batch * seq, ATTN_CFG_DIM)


def _attn_combine_body(o0, o1, o2, l0, l1, l2, d_ref):
    ls = [l0[...], l1[...], l2[...]]
    mx = jnp.maximum(jnp.maximum(ls[0], ls[1]), ls[2])
    es = [jnp.exp(l - mx) for l in ls]
    inv = 1.0 / (es[0] + es[1] + es[2])
    for gi, o_ref in enumerate((o0, o1, o2)):
        d_ref[:, gi * ATTN_CFG_DIM:(gi + 1) * ATTN_CFG_DIM] = (
            o_ref[...] * (es[gi] * inv)).astype(d_ref.dtype)


def _attn_combine(outs, lses, tm=1024):
    m = outs[0].shape[0]
    spec = pl.BlockSpec((tm, ATTN_CFG_DIM), lambda i: (i, 0))
    return pl.pallas_call(
        _attn_combine_body,
        grid=(m // tm,),
        in_specs=[spec] * 6,
        out_specs=pl.BlockSpec((tm, ATTN_DIM), lambda i: (i, 0)),
        out_shape=jax.ShapeDtypeStruct((m, ATTN_DIM), BF16),
        compiler_params=_cparams(("parallel",)),
        name="attn_combine",
    )(*outs, *lses)


def _mem_kv_body(mem_ref, g_ref, w_ref, kg_ref, k_ref, v_ref):
    d = D_MODEL
    h = _rms(mem_ref[...], g_ref[...]).astype(BF16)
    for hh in range(XATTN_HEADS):
        ls = slice(hh * XATTN_HEAD_DIM, (hh + 1) * XATTN_HEAD_DIM)
        kh = _dot(h, w_ref[:, ls])
        k_ref[:, ls] = _rms(kh, kg_ref[...]).astype(k_ref.dtype)
    v_ref[...] = _dot(h, w_ref[:, d:]).astype(v_ref.dtype)


def _mem_kv(mem, g, wkv, kn_g, batch):
    d = D_MODEL
    spec = pl.BlockSpec((None, MEM_LEN, d), lambda b: (b, 0, 0))
    return pl.pallas_call(
        _mem_kv_body,
        grid=(batch,),
        in_specs=[spec, _const_spec((1, d)), _const_spec((d, 2 * d)),
                  _const_spec((1, XATTN_HEAD_DIM))],
        out_specs=[spec, spec],
        out_shape=[jax.ShapeDtypeStruct((batch, MEM_LEN, d), BF16)] * 2,
        compiler_params=_cparams(("parallel",)),
        name="mem_kv",
    )(mem, g.reshape(1, d), wkv, kn_g.reshape(1, XATTN_HEAD_DIM))


def _post_body(x_ref, m1_ref, m2_ref, wo1_ref, wo2_ref, gx_ref, wq_ref, qg_ref, k_ref, v_ref,
               wo_ref, gm_ref, w1_ref, w2_ref, o_ref):
    x1 = x_ref[...] + _dot(m1_ref[...], wo1_ref[...]) + _dot(m2_ref[...], wo2_ref[...])
    q = _dot(_rms(x1, gx_ref[...]).astype(BF16), wq_ref[...])
    scale = np.float32(XATTN_HEAD_DIM ** -0.5)
    heads = []
    for hh in range(XATTN_HEADS):
        ls = slice(hh * XATTN_HEAD_DIM, (hh + 1) * XATTN_HEAD_DIM)
        qh = _rms(q[:, ls], qg_ref[...]).astype(BF16)
        s = _dot_nt(qh, k_ref[:, ls]) * scale
        p = jnp.exp(s - jnp.max(s, axis=-1, keepdims=True))
        p = p / jnp.sum(p, axis=-1, keepdims=True)
        heads.append(_dot(p.astype(BF16), v_ref[:, ls]).astype(BF16))
    x2 = x1 + _dot(jnp.concatenate(heads, axis=1), wo_ref[...])
    hid = jnp.maximum(_dot(_rms(x2, gm_ref[...]).astype(BF16), w1_ref[...]), 0.0)
    o_ref[...] = x2 + _dot((hid * hid).astype(BF16), w2_ref[...])


def _post(x2d, m1, m2, wo1, wo2, gx, wq, qn_g, k_mem, v_mem, wo, gm, w1, w2, seq, tm=256):
    m, d = x2d.shape
    per_b = seq // tm
    row = lambda n: pl.BlockSpec((tm, n), lambda i: (i, 0))
    mem = pl.BlockSpec((None, MEM_LEN, d), lambda i: (i // per_b, 0, 0))
    return pl.pallas_call(
        _post_body,
        grid=(m // tm,),
        in_specs=[row(d), row(m1.shape[1]), row(m2.shape[1]),
                  _const_spec(wo1.shape), _const_spec(wo2.shape), _const_spec((1, d)),
                  _const_spec(wq.shape), _const_spec((1, XATTN_HEAD_DIM)), mem, mem,
                  _const_spec(wo.shape), _const_spec((1, d)),
                  _const_spec(w1.shape), _const_spec(w2.shape)],
        out_specs=row(d),
        out_shape=jax.ShapeDtypeStruct((m, d), F32),
        compiler_params=_cparams(("parallel",)),
        name="post",
    )(x2d, m1, m2, wo1, wo2, gx.reshape(1, d), wq, qn_g.reshape(1, XATTN_HEAD_DIM),
      k_mem, v_mem, wo, gm.reshape(1, d), w1, w2)


def kernel(x, mem, norm_mix_g, ev_w_in, conv_dw_w, conv_dw_b, conv_ln_g, conv_ln_b,
           hgrn_lb_logits, hgrn_onorm_g, ev_w_out, od_w_in, sgu_ln_g, sgu_ln_b, sgu_w, sgu_b,
           attn_qnorm_g, attn_knorm_g, od_w_out, norm_xattn_g, norm_mem_g, xattn_wq, xattn_wkv,
           xattn_qnorm_g, xattn_knorm_g, xattn_wo, norm_mlp_g, mlp_w1, mlp_w2):
    batch, seq, d = x.shape
    bf = lambda a: a.astype(BF16)
    lb_all = jnp.cumsum(jax.nn.softmax(hgrn_lb_logits.astype(F32), axis=0), axis=0)
    lb_all = lb_all - lb_all[0]
    xf = x.reshape(batch * seq, d)
    for l in range(DEPTH):
        if l % 2 == 0:
            e = l // 2
            a_in, q, f_pre, i, g = _norm_proj(
                xf, norm_mix_g[l], bf(ev_w_in[e]),
                (2 * CONV_DIM, HGRN_DIM, HGRN_DIM, HGRN_DIM, HGRN_DIM), (F32,) * 5, tm=512)
            m1 = _conformer_conv(a_in, conv_dw_w[e], conv_dw_b[e], conv_ln_g[e], conv_ln_b[e],
                                 batch, seq)
            m2 = _hgrn2(q, f_pre, i, g, lb_all[e], hgrn_onorm_g[e], batch, seq)
            w_out = bf(ev_w_out[e])
            wo1, wo2 = w_out[:CONV_DIM], w_out[CONV_DIM:]
        else:
            o = l // 2
            z, q, k, v = _norm_proj(
                xf, norm_mix_g[l], bf(od_w_in[o]),
                (2 * SGU_DIM, ATTN_DIM, ATTN_DIM, ATTN_DIM), (F32, F32, F32, BF16), tm=512)
            m1 = _chunked_sgu(z, sgu_ln_g[o], sgu_ln_b[o], sgu_w[o], sgu_b[o])
            qn, kn = _qk_prep(q, k, attn_qnorm_g[o], attn_knorm_g[o], seq)
            outs, lses = [], []
            for cfg, (_, dilation) in enumerate(DIL_CONFIGS):
                o_c, l_c = _dilated_attention(qn, kn, v, cfg, dilation, batch, seq)
                outs.append(o_c)
                lses.append(l_c)
            m2 = _attn_combine(outs, lses)
            w_out = bf(od_w_out[o])
            wo1, wo2 = w_out[:SGU_DIM], w_out[SGU_DIM:]
        k_mem, v_mem = _mem_kv(mem, norm_mem_g[l], bf(xattn_wkv[l]), xattn_knorm_g[l], batch)
        xf = _post(xf, m1, m2, wo1, wo2, norm_xattn_g[l], bf(xattn_wq[l]), xattn_qnorm_g[l],
                   k_mem, v_mem, bf(xattn_wo[l]), norm_mlp_g[l], bf(mlp_w1[l]), bf(mlp_w2[l]), seq)
    return xf.reshape(batch, seq, d)
```

```python
import functools

import numpy as np
import jax
import jax.numpy as jnp
from jax import lax
from jax.experimental import pallas as pl
from jax.experimental.pallas import tpu as pltpu

F32 = jnp.float32
BF16 = jnp.bfloat16

D_MODEL = 1024
DEPTH = 4
EPS = 1e-6
CONV_DIM = 512
CONV_WIDTH = 31
HGRN_DIM = 512
HGRN_HEAD_DIM = 128
HGRN_HEADS = 4
HGRN_CHUNK = 64
SGU_DIM = 512
SGU_GROUPS = 4
SGU_CHUNK = 128
ATTN_HEAD_DIM = 64
DIL_CONFIGS = ((128, 1), (512, 4), (2048, 16))
ATTN_HEADS_PER_CFG = 4
ATTN_CFG_DIM = ATTN_HEADS_PER_CFG * ATTN_HEAD_DIM
ATTN_DIM = ATTN_CFG_DIM * len(DIL_CONFIGS)
ATTN_BLOCK = 128
ATTN_SPAN = 128
ROPE_THETA = 500000.0
ROT_DIM = 16
MEM_LEN = 256
XATTN_HEADS = 4
XATTN_HEAD_DIM = 256
D_FF = 4096

LANES = 128
SUBLANES = 8
ROW_TILE = 1024
CONV_HALO = 32
ATTN_SUPER = 1024
ATTN_PREP_ROWS = 256
NEG_BIG = -1e30
VMEM_LIMIT = 56 * 1024 * 1024


def _cparams(sem):
    return pltpu.CompilerParams(dimension_semantics=sem, vmem_limit_bytes=VMEM_LIMIT)


def _const_spec(shape):
    nd = len(shape)
    return pl.BlockSpec(shape, lambda *_: (0,) * nd, pipeline_mode=pl.Buffered(1))


def _dot(a, b):
    return jnp.dot(a, b, preferred_element_type=F32)


def _dot_nt(a, b):
    return lax.dot_general(a, b, (((1,), (1,)), ((), ())), preferred_element_type=F32)


def _dot_tn(a, b):
    return lax.dot_general(a, b, (((0,), (0,)), ((), ())), preferred_element_type=F32)


def _rms(x, g):
    return x * lax.rsqrt(jnp.mean(x * x, axis=-1, keepdims=True) + EPS) * g


def _layernorm(x, g, b):
    mu = jnp.mean(x, axis=-1, keepdims=True)
    xc = x - mu
    return xc * lax.rsqrt(jnp.mean(xc * xc, axis=-1, keepdims=True) + EPS) * g + b


def _sigmoid(x):
    return 1.0 / (1.0 + jnp.exp(-x))


def _norm_proj_body(x_ref, g_ref, w_ref, o_ref, h_ref):
    @pl.when(pl.program_id(1) == 0)
    def _():
        h_ref[...] = _rms(x_ref[...], g_ref[...]).astype(BF16)

    o_ref[...] = _dot(h_ref[...], w_ref[...])


def _norm_proj(x2d, g, w, tn, tm=ROW_TILE):
    m, d = x2d.shape
    n_total = w.shape[1]
    return pl.pallas_call(
        _norm_proj_body,
        grid=(m // tm, n_total // tn),
        in_specs=[pl.BlockSpec((tm, d), lambda i, j: (i, 0)),
                  _const_spec((1, d)),
                  pl.BlockSpec((d, tn), lambda i, j: (0, j))],
        out_specs=pl.BlockSpec((tm, tn), lambda i, j: (i, j)),
        out_shape=jax.ShapeDtypeStruct((m, n_total), F32),
        scratch_shapes=[pltpu.VMEM((tm, d), BF16)],
        compiler_params=_cparams(("parallel", "arbitrary")),
        name="norm_proj",
    )(x2d, g.reshape(1, d), w)


def _conv_body(a_ref, w_ref, b_ref, lg_ref, lb_ref, o_ref, hs_ref, *, tt, rs):
    c = CONV_DIM
    n_rows = CONV_HALO + tt

    @pl.when(pl.program_id(1) == 0)
    def _():
        hs_ref[0, 0:CONV_HALO, :] = jnp.zeros((CONV_HALO, c), F32)

    hs_ref[0, CONV_HALO:n_rows, :] = a_ref[:, :c] * _sigmoid(a_ref[:, c:])
    for s in range(1, SUBLANES):
        hs_ref[s, 0:n_rows - SUBLANES, :] = hs_ref[0, s:s + n_rows - SUBLANES, :]
    base = CONV_HALO - (CONV_WIDTH - 1)
    for r0 in range(0, tt, rs):
        acc = jnp.zeros((rs, c), F32) + b_ref[...]
        for j in range(CONV_WIDTH):
            s = (base + j) % SUBLANES
            a0 = base + j - s + r0
            acc = acc + w_ref[j:j + 1, :] * hs_ref[s, a0:a0 + rs, :]
        y = _layernorm(acc, lg_ref[...], lb_ref[...])
        o_ref[r0:r0 + rs, :] = (y * _sigmoid(y)).astype(o_ref.dtype)
    hs_ref[0, 0:CONV_HALO, :] = hs_ref[0, tt:n_rows, :]


def _conformer_conv(p, dw_w, dw_b, ln_g, ln_b, batch, seq, tt=512, rs=64):
    c = CONV_DIM
    a3 = p.reshape(batch, seq, p.shape[1])
    w_pad = jnp.zeros((CONV_HALO, c), F32).at[:CONV_WIDTH].set(dw_w)
    out = pl.pallas_call(
        functools.partial(_conv_body, tt=tt, rs=rs),
        grid=(batch, seq // tt),
        in_specs=[pl.BlockSpec((None, tt, 2 * c), lambda b, t: (b, t, 0)),
                  _const_spec((CONV_HALO, c)), _const_spec((1, c)),
                  _const_spec((1, c)), _const_spec((1, c))],
        out_specs=pl.BlockSpec((None, tt, c), lambda b, t: (b, t, 0)),
        out_shape=jax.ShapeDtypeStruct((batch, seq, c), BF16),
        scratch_shapes=[pltpu.VMEM((SUBLANES, CONV_HALO + tt, c), F32)],
        compiler_params=_cparams(("parallel", "arbitrary")),
        name="conformer_conv",
    )(a3, w_pad, dw_b.reshape(1, c), ln_g.reshape(1, c), ln_b.reshape(1, c))
    return out.reshape(batch * seq, c)


_HGRN_LEVELS = (32, 16, 8, 4, 2, 1)


def _hgrn_tables():
    n = HGRN_CHUNK
    t = np.arange(n)[:, None]
    u = np.arange(n)[None, :]
    blocks = [u <= t, u > t]
    masks = []
    for m in _HGRN_LEVELS:
        c = (t // (2 * m)) * (2 * m)
        upper = (t - c) >= m
        blocks.append(upper & (u >= c + m) & (u <= t))
        blocks.append((~upper) & (u >= t + 1) & (u <= c + m - 1))
        cs = (u // (2 * m)) * (2 * m)
        masks.append(((t // (2 * m)) == (u // (2 * m))) & upper & ((u - cs) < m))
    d = np.concatenate(blocks, axis=0).astype(np.float32)
    return np.concatenate([d, d, d], axis=1), np.stack(masks).astype(np.float32)


def _hgrn_body(q_ref, f_ref, i_ref, g_ref, lb_ref, og_ref, d_ref, m_ref, o_ref, st_ref,
               *, n_chunks):
    n = HGRN_CHUNK
    hd = HGRN_HEAD_DIM

    @pl.when(pl.program_id(1) == 0)
    def _():
        st_ref[...] = jnp.zeros(st_ref.shape, F32)

    lb = lb_ref[...]

    def chunk(ci, carry):
        rows = pl.ds(pl.multiple_of(ci * n, n), n)
        f = lb + (1.0 - lb) * _sigmoid(f_ref[rows, :])
        logf = jnp.log(f)
        kk = 1.0 - f
        hi = logf.astype(BF16)
        r1 = logf - hi.astype(F32)
        mid = r1.astype(BF16)
        lo = (r1 - mid.astype(F32)).astype(BF16)
        expo = _dot(d_ref[...], jnp.concatenate([hi, mid, lo], axis=0))
        q = q_ref[rows, :]
        v = i_ref[rows, :]
        g = g_ref[rows, :]
        for h in range(HGRN_HEADS):
            ls = slice(h * hd, (h + 1) * hd)
            qh, kh, vh, gh = q[:, ls], kk[:, ls], v[:, ls], g[:, ls]
            eh = expo[:, ls]
            e_b = eh[0:n]
            e_bl = eh[n:2 * n]
            st = st_ref[h]
            vb = vh.astype(BF16)
            o = _dot_nt((qh * jnp.exp(e_b)).astype(BF16), st.astype(BF16))
            sc = jnp.zeros((n, n), F32)
            for li in range(len(_HGRN_LEVELS)):
                r0 = (2 + 2 * li) * n
                qs = (qh * jnp.exp(eh[r0:r0 + n])).astype(BF16)
                ks = (kh * jnp.exp(eh[r0 + n:r0 + 2 * n])).astype(BF16)
                sc = sc + m_ref[li] * _dot_nt(qs, ks)
            o = o + _dot(sc.astype(BF16), vb)
            o = o + jnp.sum(qh * kh, axis=-1, keepdims=True) * vh
            dec = jnp.exp(e_b[n - 1:n, :])
            st_ref[h] = st * dec + _dot_tn(vb, (kh * jnp.exp(e_bl)).astype(BF16))
            on = _rms(o, og_ref[...])
            o_ref[rows, ls] = (on * (gh * _sigmoid(gh))).astype(o_ref.dtype)
        return carry

    lax.fori_loop(0, n_chunks, chunk, 0, unroll=2)


def _hgrn2(p, lb, onorm_g, batch, seq, tt=512):
    c = HGRN_DIM
    d3, masks = _hgrn_tables()
    p3 = p.reshape(batch, seq, p.shape[1])
    first = 2 * CONV_DIM // c
    col = lambda j: pl.BlockSpec((None, tt, c), lambda b, t: (b, t, first + j))
    out = pl.pallas_call(
        functools.partial(_hgrn_body, n_chunks=tt // HGRN_CHUNK),
        grid=(batch, seq // tt),
        in_specs=[col(0), col(1), col(2), col(3),
                  _const_spec((1, c)), _const_spec((1, HGRN_HEAD_DIM)),
                  _const_spec(d3.shape), _const_spec(masks.shape)],
        out_specs=pl.BlockSpec((None, tt, c), lambda b, t: (b, t, 0)),
        out_shape=jax.ShapeDtypeStruct((batch, seq, c), BF16),
        scratch_shapes=[pltpu.VMEM((HGRN_HEADS, HGRN_HEAD_DIM, HGRN_HEAD_DIM), F32)],
        compiler_params=_cparams(("parallel", "arbitrary")),
        name="hgrn2",
    )(p3, p3, p3, p3, lb.reshape(1, c), onorm_g.reshape(1, HGRN_HEAD_DIM),
      jnp.asarray(d3, BF16), jnp.asarray(masks, F32))
    return out.reshape(batch * seq, c)


def _sgu_body(z_ref, lg_ref, lb_ref, w_ref, bias_ref, o_ref, *, tt):
    c = SGU_DIM
    gw = c // SGU_GROUPS
    z = z_ref[...]
    z = 0.5 * z * (1.0 + lax.erf(z * np.float32(1.0 / np.sqrt(2.0))))
    u = z[:, :c]
    v = _layernorm(z[:, c:], lg_ref[...], lb_ref[...]).astype(BF16)
    for ci in range(tt // SGU_CHUNK):
        rs = slice(ci * SGU_CHUNK, (ci + 1) * SGU_CHUNK)
        for gi in range(SGU_GROUPS):
            ls = slice(gi * gw, (gi + 1) * gw)
            mixed = _dot(w_ref[gi], v[rs, ls]) + bias_ref[:, ls]
            o_ref[rs, ls] = (u[rs, ls] * mixed).astype(o_ref.dtype)


def _chunked_sgu(p, ln_g, ln_b, w_s, b_s, tt=512):
    m = p.shape[0]
    c = SGU_DIM
    tril = np.tril(np.ones((SGU_CHUNK, SGU_CHUNK), dtype=bool))
    w = jnp.where(tril[None], w_s, 0.0).astype(BF16)
    bias = jnp.repeat(b_s.T, c // SGU_GROUPS, axis=1)
    return pl.pallas_call(
        functools.partial(_sgu_body, tt=tt),
        grid=(m // tt,),
        in_specs=[pl.BlockSpec((tt, 2 * c), lambda i: (i, 0)),
                  _const_spec((1, c)), _const_spec((1, c)),
                  _const_spec(w.shape), _const_spec((SGU_CHUNK, c))],
        out_specs=pl.BlockSpec((tt, c), lambda i: (i, 0)),
        out_shape=jax.ShapeDtypeStruct((m, c), BF16),
        compiler_params=_cparams(("parallel",)),
        name="chunked_sgu",
    )(p, ln_g.reshape(1, c), ln_b.reshape(1, c), w, bias)


def _rope_tables(seq):
    half = ROT_DIM // 2
    inv_freq = jnp.power(ROPE_THETA, -jnp.arange(half, dtype=F32) / half)
    ang = jnp.arange(seq, dtype=F32)[:, None] * inv_freq[None, :]
    cos, sin = jnp.cos(ang), jnp.sin(ang)
    ones = jnp.ones((seq, ATTN_HEAD_DIM - ROT_DIM), F32)
    zeros = jnp.zeros((seq, ATTN_HEAD_DIM - ROT_DIM), F32)
    zh = jnp.zeros((seq, half), F32)
    c_head = jnp.concatenate([cos, cos, ones], axis=1)
    s_dn_head = jnp.concatenate([zh, sin, zeros], axis=1)
    s_up_head = jnp.concatenate([-sin, zh, zeros], axis=1)
    tile = lambda a: jnp.concatenate([a, a], axis=1)
    return tile(c_head), tile(s_dn_head), tile(s_up_head)


def _attn_geometry(dilation):
    per_class = ATTN_SUPER // dilation
    rows = min(per_class, ATTN_BLOCK)
    return per_class, rows, per_class // rows


def _attn_prep(x_ref, g_ref, c_ref, sd_ref, su_ref, bd_ref, dst_ref):
    half = ROT_DIM // 2
    for pair in range(ATTN_CFG_DIM // LANES):
        ls = slice(pair * LANES, (pair + 1) * LANES)
        for r0 in range(0, ATTN_SUPER, ATTN_PREP_ROWS):
            rr = slice(r0, r0 + ATTN_PREP_ROWS)
            x = x_ref[rr, ls]
            ms = _dot((x * x).astype(BF16), bd_ref[...])
            y = x * lax.rsqrt(ms + EPS) * g_ref[:, ls]
            dst_ref[pair, rr, :] = (y * c_ref[rr, :] + pltpu.roll(y, half, 1) * sd_ref[rr, :]
                                    + pltpu.roll(y, LANES - half, 1) * su_ref[rr, :])


def _attn_unit(cfg, dilation, r, j, step, qs_ref, kd_ref, vd_ref, oacc_ref, lacc_ref):
    per_class, rows, _ = _attn_geometry(dilation)
    blk = ATTN_BLOCK
    start = j * (blk * dilation) + r
    if dilation == 1:
        q_rows = pl.ds(pl.multiple_of(start, blk), rows)
    else:
        q_rows = pl.ds(start, rows, stride=dilation)
    k_rows = pl.ds(pl.multiple_of(j * blk, blk), blk + rows)
    qi = lax.broadcasted_iota(jnp.int32, (rows, blk + rows), 0)
    cj = lax.broadcasted_iota(jnp.int32, (rows, blk + rows), 1)
    dist = qi - (cj - blk)
    first_key = jnp.maximum(blk - (step * per_class + j * blk), 0)
    valid = (dist >= 0) & (dist <= ATTN_SPAN) & (cj >= first_key)
    lane = lax.broadcasted_iota(jnp.int32, (1, LANES), 1)
    for pair in range(ATTN_CFG_DIM // LANES):
        qp = qs_ref[pair, q_rows, :].astype(BF16)
        kw = kd_ref[pair, r, k_rows, :]
        vw = vd_ref[pair, r, k_rows, :]
        o_pair = None
        l_pair = None
        for hh in range(LANES // ATTN_HEAD_DIM):
            in_head = (lane // ATTN_HEAD_DIM) == hh
            s = _dot_nt(jnp.where(in_head, qp, jnp.zeros_like(qp)), kw)
            s = jnp.where(valid, s, NEG_BIG)
            mx = jnp.max(s, axis=-1, keepdims=True)
            p = jnp.exp(s - mx)
            den = jnp.sum(p, axis=-1, keepdims=True)
            o_h = _dot(p.astype(BF16), vw) * (1.0 / den)
            lse = mx + jnp.log(den)
            o_pair = o_h if o_pair is None else jnp.where(in_head, o_h, o_pair)
            l_pair = (jnp.broadcast_to(lse, (rows, LANES)) if l_pair is None
                      else jnp.where(in_head, lse, l_pair))
        oacc_ref[cfg, pair, q_rows, :] = o_pair
        lacc_ref[cfg, pair, q_rows, :] = l_pair


def _attn_config(cfg, dilation, step, ks_ref, v_refs, qs_ref, kd_ref, vd_ref,
                 oacc_ref, lacc_ref):
    per_class, _, n_blk = _attn_geometry(dilation)
    blk = ATTN_BLOCK
    pairs = ATTN_CFG_DIM // LANES

    @pl.when(step == 0)
    def _():
        zeros = jnp.zeros((pairs, dilation, blk, LANES), BF16)
        kd_ref[:, :, 0:blk, :] = zeros
        vd_ref[:, :, 0:blk, :] = zeros

    for pair in range(pairs):
        for r in range(dilation):
            src = pl.ds(r, per_class, stride=dilation) if dilation > 1 else slice(0, per_class)
            kd_ref[pair, r, blk:blk + per_class, :] = ks_ref[pair, src, :].astype(BF16)
            vd_ref[pair, r, blk:blk + per_class, :] = v_refs[pair][src, :].astype(BF16)

    def unit(u, carry):
        _attn_unit(cfg, dilation, u % dilation, u // dilation, step,
                   qs_ref, kd_ref, vd_ref, oacc_ref, lacc_ref)
        return carry

    lax.fori_loop(0, dilation * n_blk, unit, 0)

    for ref in (kd_ref, vd_ref):
        ref[:, :, 0:blk, :] = ref[:, :, per_class:per_class + blk, :]


def _dil_attn_body(q_ref, k_ref, va_ref, vb_ref, qg_ref, kg_ref, c_ref, sd_ref, su_ref, bd_ref,
                   o_ref, qs_ref, ks_ref, kd0, kd1, kd2, vd0, vd1, vd2, oacc_ref, lacc_ref):
    step = pl.program_id(1)
    cfg_id = pl.program_id(2)
    n_cfg = len(DIL_CONFIGS)
    _attn_prep(q_ref, qg_ref, c_ref, sd_ref, su_ref, bd_ref, qs_ref)
    _attn_prep(k_ref, kg_ref, c_ref, sd_ref, su_ref, bd_ref, ks_ref)
    kds, vds = (kd0, kd1, kd2), (vd0, vd1, vd2)
    for cfg, (_, dilation) in enumerate(DIL_CONFIGS):
        @pl.when(cfg_id == cfg)
        def _(cfg=cfg, dilation=dilation):
            _attn_config(cfg, dilation, step, ks_ref, (va_ref, vb_ref), qs_ref,
                         kds[cfg], vds[cfg], oacc_ref, lacc_ref)

    @pl.when(cfg_id == n_cfg - 1)
    def _():
        for pair in range(ATTN_CFG_DIM // LANES):
            for r0 in range(0, ATTN_SUPER, ATTN_PREP_ROWS):
                rr = slice(r0, r0 + ATTN_PREP_ROWS)
                ls = [lacc_ref[c, pair, rr, :] for c in range(n_cfg)]
                mx = jnp.maximum(jnp.maximum(ls[0], ls[1]), ls[2])
                es = [jnp.exp(l - mx) for l in ls]
                inv = 1.0 / (es[0] + es[1] + es[2])
                for c in range(n_cfg):
                    col = c * ATTN_CFG_DIM + pair * LANES
                    o_ref[rr, col:col + LANES] = (
                        oacc_ref[c, pair, rr, :] * (es[c] * inv)).astype(o_ref.dtype)


def _dilated_attention(p, qn_g, kn_g, batch, seq):
    m = p.shape[0]
    n_cfg = len(DIL_CONFIGS)
    n_sb = seq // ATTN_SUPER
    pairs = ATTN_CFG_DIM // LANES
    c_t, sd_t, su_t = _rope_tables(seq)
    head = np.arange(LANES) // ATTN_HEAD_DIM
    bd = jnp.asarray((head[:, None] == head[None, :]).astype(np.float32) / ATTN_HEAD_DIM, BF16)
    tile_g = lambda g: jnp.tile(g, ATTN_CFG_DIM // ATTN_HEAD_DIM).reshape(1, ATTN_CFG_DIM)
    qg = tile_g(qn_g) * np.float32(ATTN_HEAD_DIM ** -0.5)
    q0 = 2 * SGU_DIM // ATTN_CFG_DIM
    k0 = q0 + n_cfg
    v0 = (2 * SGU_DIM + 2 * ATTN_DIM) // LANES
    wide = lambda c0: pl.BlockSpec((ATTN_SUPER, ATTN_CFG_DIM),
                                   lambda b, s, c: (b * n_sb + s, c0 + c))
    v_spec = lambda pair: pl.BlockSpec((ATTN_SUPER, LANES),
                                       lambda b, s, c: (b * n_sb + s, v0 + pairs * c + pair))
    tab = pl.BlockSpec((ATTN_SUPER, LANES), lambda b, s, c: (s, 0))
    hist = lambda d: pltpu.VMEM((pairs, d, ATTN_BLOCK + ATTN_SUPER // d, LANES), BF16)
    dils = [d for _, d in DIL_CONFIGS]
    return pl.pallas_call(
        _dil_attn_body,
        grid=(batch, n_sb, n_cfg),
        in_specs=[wide(q0), wide(k0), v_spec(0), v_spec(1),
                  _const_spec((1, ATTN_CFG_DIM)), _const_spec((1, ATTN_CFG_DIM)),
                  tab, tab, tab, _const_spec((LANES, LANES))],
        out_specs=pl.BlockSpec((ATTN_SUPER, ATTN_DIM), lambda b, s, c: (b * n_sb + s, 0)),
        out_shape=jax.ShapeDtypeStruct((m, ATTN_DIM), BF16),
        scratch_shapes=[pltpu.VMEM((pairs, ATTN_SUPER, LANES), F32),
                        pltpu.VMEM((pairs, ATTN_SUPER, LANES), F32)]
        + [hist(d) for d in dils] + [hist(d) for d in dils]
        + [pltpu.VMEM((n_cfg, pairs, ATTN_SUPER, LANES), F32)] * 2,
        compiler_params=_cparams(("parallel", "arbitrary", "arbitrary")),
        name="dil_attn",
    )(p, p, p, p, qg, tile_g(kn_g), c_t, sd_t, su_t, bd)


def _mem_kv_body(mem_ref, g_ref, w_ref, kg_ref, k_ref, v_ref):
    d = D_MODEL
    h = _rms(mem_ref[...], g_ref[...]).astype(BF16)
    for hh in range(XATTN_HEADS):
        ls = slice(hh * XATTN_HEAD_DIM, (hh + 1) * XATTN_HEAD_DIM)
        kh = _dot(h, w_ref[:, ls])
        k_ref[:, ls] = _rms(kh, kg_ref[...]).astype(k_ref.dtype)
    v_ref[...] = _dot(h, w_ref[:, d:]).astype(v_ref.dtype)


def _mem_kv(mem, g, wkv, kn_g, batch):
    d = D_MODEL
    spec = pl.BlockSpec((None, MEM_LEN, d), lambda b: (b, 0, 0))
    return pl.pallas_call(
        _mem_kv_body,
        grid=(batch,),
        in_specs=[spec, _const_spec((1, d)), _const_spec((d, 2 * d)),
                  _const_spec((1, XATTN_HEAD_DIM))],
        out_specs=[spec, spec],
        out_shape=[jax.ShapeDtypeStruct((batch, MEM_LEN, d), BF16)] * 2,
        compiler_params=_cparams(("parallel",)),
        name="mem_kv",
    )(mem, g.reshape(1, d), wkv, kn_g.reshape(1, XATTN_HEAD_DIM))


def _xattn_body(x_ref, m1_ref, m2_ref, wo1_ref, wo2_ref, gx_ref, wq_ref, qg_ref, k_ref, v_ref,
                wo_ref, o_ref):
    x1 = x_ref[...] + _dot(m1_ref[...], wo1_ref[...]) + _dot(m2_ref[...], wo2_ref[...])
    q = _dot(_rms(x1, gx_ref[...]).astype(BF16), wq_ref[...])
    heads = []
    for hh in range(XATTN_HEADS):
        ls = slice(hh * XATTN_HEAD_DIM, (hh + 1) * XATTN_HEAD_DIM)
        qh = _rms(q[:, ls], qg_ref[...]).astype(BF16)
        s = _dot_nt(qh, k_ref[:, ls])
        p = jnp.exp(s - jnp.max(s, axis=-1, keepdims=True))
        inv = 1.0 / jnp.sum(p, axis=-1, keepdims=True)
        heads.append((_dot(p.astype(BF16), v_ref[:, ls]) * inv).astype(BF16))
    o_ref[...] = x1 + _dot(jnp.concatenate(heads, axis=1), wo_ref[...])


def _xattn(x2d, m1, m2, wo1, wo2, gx, wq, qn_g, k_mem, v_mem, wo, seq, tm=ROW_TILE):
    m, d = x2d.shape
    per_b = seq // tm
    row = lambda n: pl.BlockSpec((tm, n), lambda i: (i, 0))
    mem = pl.BlockSpec((None, MEM_LEN, d), lambda i: (i // per_b, 0, 0))
    qg = (qn_g * np.float32(XATTN_HEAD_DIM ** -0.5)).reshape(1, XATTN_HEAD_DIM)
    return pl.pallas_call(
        _xattn_body,
        grid=(m // tm,),
        in_specs=[row(d), row(m1.shape[1]), row(m2.shape[1]),
                  _const_spec(wo1.shape), _const_spec(wo2.shape), _const_spec((1, d)),
                  _const_spec(wq.shape), _const_spec((1, XATTN_HEAD_DIM)), mem, mem,
                  _const_spec(wo.shape)],
        out_specs=row(d),
        out_shape=jax.ShapeDtypeStruct((m, d), F32),
        compiler_params=_cparams(("parallel",)),
        name="xattn",
    )(x2d, m1, m2, wo1, wo2, gx.reshape(1, d), wq, qg, k_mem, v_mem, wo)


def _mlp_body(x_ref, g_ref, w1_ref, w2_ref, o_ref, *, ff_tile):
    x = x_ref[...]
    h = _rms(x, g_ref[...]).astype(BF16)
    acc = x
    for c0 in range(0, D_FF, ff_tile):
        hid = jnp.maximum(_dot(h, w1_ref[:, c0:c0 + ff_tile]), 0.0)
        acc = acc + _dot((hid * hid).astype(BF16), w2_ref[c0:c0 + ff_tile, :])
    o_ref[...] = acc


def _mlp(x2d, g, w1, w2, tm=ROW_TILE, ff_tile=1024):
    m, d = x2d.shape
    row = pl.BlockSpec((tm, d), lambda i: (i, 0))
    return pl.pallas_call(
        functools.partial(_mlp_body, ff_tile=ff_tile),
        grid=(m // tm,),
        in_specs=[row, _const_spec((1, d)), _const_spec(w1.shape), _const_spec(w2.shape)],
        out_specs=row,
        out_shape=jax.ShapeDtypeStruct((m, d), F32),
        compiler_params=_cparams(("parallel",)),
        name="mlp",
    )(x2d, g.reshape(1, d), w1, w2)


def kernel(x, mem, norm_mix_g, ev_w_in, conv_dw_w, conv_dw_b, conv_ln_g, conv_ln_b,
           hgrn_lb_logits, hgrn_onorm_g, ev_w_out, od_w_in, sgu_ln_g, sgu_ln_b, sgu_w, sgu_b,
           attn_qnorm_g, attn_knorm_g, od_w_out, norm_xattn_g, norm_mem_g, xattn_wq, xattn_wkv,
           xattn_qnorm_g, xattn_knorm_g, xattn_wo, norm_mlp_g, mlp_w1, mlp_w2):
    batch, seq, d = x.shape
    bf = lambda a: a.astype(BF16)
    lb_all = jnp.cumsum(jax.nn.softmax(hgrn_lb_logits.astype(F32), axis=0), axis=0)
    lb_all = lb_all - lb_all[0]
    xf = x.reshape(batch * seq, d)
    for l in range(DEPTH):
        if l % 2 == 0:
            e = l // 2
            w_in = bf(ev_w_in[e])
            p = _norm_proj(xf, norm_mix_g[l], w_in, tn=w_in.shape[1] // 3)
            m1 = _conformer_conv(p, conv_dw_w[e], conv_dw_b[e], conv_ln_g[e], conv_ln_b[e],
                                 batch, seq)
            m2 = _hgrn2(p, lb_all[e], hgrn_onorm_g[e], batch, seq)
            w_out = bf(ev_w_out[e])
            wo1, wo2 = w_out[:CONV_DIM], w_out[CONV_DIM:]
        else:
            o = l // 2
            w_in = bf(od_w_in[o])
            p = _norm_proj(xf, norm_mix_g[l], w_in, tn=w_in.shape[1] // 2)
            m1 = _chunked_sgu(p, sgu_ln_g[o], sgu_ln_b[o], sgu_w[o], sgu_b[o])
            m2 = _dilated_attention(p, attn_qnorm_g[o], attn_knorm_g[o], batch, seq)
            w_out = bf(od_w_out[o])
            wo1, wo2 = w_out[:SGU_DIM], w_out[SGU_DIM:]
        k_mem, v_mem = _mem_kv(mem, norm_mem_g[l], bf(xattn_wkv[l]), xattn_knorm_g[l], batch)
        xf = _xattn(xf, m1, m2, wo1, wo2, norm_xattn_g[l], bf(xattn_wq[l]), xattn_qnorm_g[l],
                    k_mem, v_mem, bf(xattn_wo[l]), seq)
        xf = _mlp(xf, norm_mlp_g[l], bf(mlp_w1[l]), bf(mlp_w2[l]))
    return xf.reshape(batch, seq, d)
```

```python
import functools

import numpy as np
import jax
import jax.numpy as jnp
from jax import lax
from jax.experimental import pallas as pl
from jax.experimental.pallas import tpu as pltpu

F32 = jnp.float32
BF16 = jnp.bfloat16

D_MODEL = 1024
DEPTH = 4
EPS = 1e-6
CONV_DIM = 512
CONV_WIDTH = 31
HGRN_DIM = 512
HGRN_HEAD_DIM = 128
HGRN_HEADS = 4
HGRN_CHUNK = 64
SGU_DIM = 512
SGU_GROUPS = 4
SGU_CHUNK = 128
ATTN_HEAD_DIM = 64
DIL_CONFIGS = ((128, 1), (512, 4), (2048, 16))
ATTN_HEADS_PER_CFG = 4
ATTN_CFG_DIM = ATTN_HEADS_PER_CFG * ATTN_HEAD_DIM
ATTN_DIM = ATTN_CFG_DIM * len(DIL_CONFIGS)
ATTN_BLOCK = 128
ATTN_SPAN = 128
ROPE_THETA = 500000.0
ROT_DIM = 16
MEM_LEN = 256
XATTN_HEADS = 4
XATTN_HEAD_DIM = 256
D_FF = 4096

LANES = 128
SUBLANES = 8
ROW_TILE = 1024
CONV_HALO = 32
ATTN_SUPER = 1024
ATTN_PREP_ROWS = 256
ATTN_GROUP = 4
NEG_BIG = -1e30
VMEM_LIMIT = 56 * 1024 * 1024


def _cparams(sem):
    return pltpu.CompilerParams(dimension_semantics=sem, vmem_limit_bytes=VMEM_LIMIT)


def _const_spec(shape):
    nd = len(shape)
    return pl.BlockSpec(shape, lambda *_: (0,) * nd, pipeline_mode=pl.Buffered(1))


def _dot(a, b):
    return jnp.dot(a, b, preferred_element_type=F32)


def _dot_nt(a, b):
    return lax.dot_general(a, b, (((1,), (1,)), ((), ())), preferred_element_type=F32)


def _dot_tn(a, b):
    return lax.dot_general(a, b, (((0,), (0,)), ((), ())), preferred_element_type=F32)


def _rms(x, g):
    return x * lax.rsqrt(jnp.mean(x * x, axis=-1, keepdims=True) + EPS) * g


def _layernorm(x, g, b):
    mu = jnp.mean(x, axis=-1, keepdims=True)
    xc = x - mu
    return xc * lax.rsqrt(jnp.mean(xc * xc, axis=-1, keepdims=True) + EPS) * g + b


def _sigmoid(x):
    return 1.0 / (1.0 + jnp.exp(-x))


def _norm_proj_body(x_ref, g_ref, w_ref, *o_refs, pieces):
    h = _rms(x_ref[...], g_ref[...]).astype(BF16)
    for out, w0, o0, n in pieces:
        o_refs[out][:, o0:o0 + n] = _dot(h, w_ref[:, w0:w0 + n]).astype(o_refs[out].dtype)


def _norm_proj(x2d, g, w, outs, pieces, tm=ROW_TILE):
    m, d = x2d.shape
    return pl.pallas_call(
        functools.partial(_norm_proj_body, pieces=pieces),
        grid=(m // tm,),
        in_specs=[pl.BlockSpec((tm, d), lambda i: (i, 0)),
                  _const_spec((1, d)), _const_spec(w.shape)],
        out_specs=[pl.BlockSpec((tm, n), lambda i: (i, 0)) for n, _ in outs],
        out_shape=[jax.ShapeDtypeStruct((m, n), dt) for n, dt in outs],
        compiler_params=_cparams(("parallel",)),
        name="norm_proj",
    )(x2d, g.reshape(1, d), w)


EVEN_OUTS = ((2 * CONV_DIM + 3 * HGRN_DIM, BF16), (HGRN_DIM, F32))
EVEN_PIECES = ((0, 0, 0, 1024), (0, 1024, 1024, 512), (1, 1536, 0, 512), (0, 2048, 1536, 1024))
ODD_IN = 2 * SGU_DIM + 3 * ATTN_DIM
ODD_OUTS = ((ODD_IN, BF16),)
ODD_PIECES = ((0, 0, 0, 1024), (0, 1024, 1024, 768), (0, 1792, 1792, 768), (0, 2560, 2560, 768))


def _conv_body(a_ref, w_ref, b_ref, lg_ref, lb_ref, o_ref, hs_ref, *, tt, rs):
    c = CONV_DIM
    n_rows = CONV_HALO + tt

    @pl.when(pl.program_id(1) == 0)
    def _():
        hs_ref[0, 0:CONV_HALO, :] = jnp.zeros((CONV_HALO, c), F32)

    hs_ref[0, CONV_HALO:n_rows, :] = (a_ref[:, :c].astype(F32)
                                      * _sigmoid(a_ref[:, c:].astype(F32)))
    for s in range(1, SUBLANES):
        hs_ref[s, 0:n_rows - SUBLANES, :] = hs_ref[0, s:s + n_rows - SUBLANES, :]
    base = CONV_HALO - (CONV_WIDTH - 1)
    for r0 in range(0, tt, rs):
        acc = jnp.zeros((rs, c), F32) + b_ref[...]
        for j in range(CONV_WIDTH):
            s = (base + j) % SUBLANES
            a0 = base + j - s + r0
            acc = acc + w_ref[j:j + 1, :] * hs_ref[s, a0:a0 + rs, :]
        y = _layernorm(acc, lg_ref[...], lb_ref[...])
        o_ref[r0:r0 + rs, :] = (y * _sigmoid(y)).astype(o_ref.dtype)
    hs_ref[0, 0:CONV_HALO, :] = hs_ref[0, tt:n_rows, :]


def _conformer_conv(p, dw_w, dw_b, ln_g, ln_b, batch, seq, tt=512, rs=64):
    c = CONV_DIM
    a3 = p.reshape(batch, seq, p.shape[1])
    w_pad = jnp.zeros((CONV_HALO, c), F32).at[:CONV_WIDTH].set(dw_w)
    out = pl.pallas_call(
        functools.partial(_conv_body, tt=tt, rs=rs),
        grid=(batch, seq // tt),
        in_specs=[pl.BlockSpec((None, tt, 2 * c), lambda b, t: (b, t, 0)),
                  _const_spec((CONV_HALO, c)), _const_spec((1, c)),
                  _const_spec((1, c)), _const_spec((1, c))],
        out_specs=pl.BlockSpec((None, tt, c), lambda b, t: (b, t, 0)),
        out_shape=jax.ShapeDtypeStruct((batch, seq, c), BF16),
        scratch_shapes=[pltpu.VMEM((SUBLANES, CONV_HALO + tt, c), F32)],
        compiler_params=_cparams(("parallel", "arbitrary")),
        name="conformer_conv",
    )(a3, w_pad, dw_b.reshape(1, c), ln_g.reshape(1, c), ln_b.reshape(1, c))
    return out.reshape(batch * seq, c)


_HGRN_LEVELS = (32, 16, 8, 4, 2, 1)
HGRN_GROUP = 2


def _hgrn_tables():
    n = HGRN_CHUNK
    t = np.arange(n)[:, None]
    u = np.arange(n)[None, :]
    blocks = [u <= t, u > t]
    masks = []
    offsets = []
    row = 2 * n
    for m in _HGRN_LEVELS:
        c = (t // (2 * m)) * (2 * m)
        upper = (t - c) >= m
        dq = upper & (u >= c + m) & (u <= t)
        dk = (~upper) & (u >= t + 1) & (u <= c + m - 1)
        if m >= SUBLANES:
            dq, dk = dq[upper[:, 0]], dk[~upper[:, 0]]
        offsets.append((row, row + dq.shape[0]))
        row += dq.shape[0] + dk.shape[0]
        blocks += [dq, dk]
        cs = (u // (2 * m)) * (2 * m)
        masks.append(((t // (2 * m)) == (u // (2 * m))) & upper & ((u - cs) < m))
    d = np.concatenate(blocks, axis=0).astype(np.float32)
    return np.concatenate([d, d, d], axis=1), np.stack(masks).astype(np.float32), tuple(offsets)


_HGRN_OFFSETS = _hgrn_tables()[2]


def _hgrn_body(q_ref, f_ref, i_ref, g_ref, lb_ref, og_ref, d_ref, m_ref, o_ref, st_ref,
               *, n_chunks):
    n = HGRN_CHUNK
    hd = HGRN_HEAD_DIM

    @pl.when(pl.program_id(1) == 0)
    def _():
        st_ref[...] = jnp.zeros(st_ref.shape, F32)

    lb = lb_ref[...]
    half = n // 2

    def level_operands(qh, kh, eh, li, m):
        q0, k0 = _HGRN_OFFSETS[li]
        if m < SUBLANES:
            return ((qh * jnp.exp(eh[q0:q0 + n])).astype(BF16),
                    (kh * jnp.exp(eh[k0:k0 + n])).astype(BF16))
        eq = jnp.exp(eh[q0:q0 + half])
        ek = jnp.exp(eh[k0:k0 + half])
        q_parts, k_parts = [], []
        for b in range(half // m):
            lo = slice(2 * b * m, (2 * b + 1) * m)
            up = slice((2 * b + 1) * m, (2 * b + 2) * m)
            cs = slice(b * m, (b + 1) * m)
            q_parts += [qh[lo], qh[up] * eq[cs]]
            k_parts += [kh[lo] * ek[cs], kh[up]]
        return (jnp.concatenate(q_parts, axis=0).astype(BF16),
                jnp.concatenate(k_parts, axis=0).astype(BF16))

    def group(gi, carry):
        chunks = []
        for ci in range(HGRN_GROUP):
            rows = pl.ds(pl.multiple_of((gi * HGRN_GROUP + ci) * n, n), n)
            f = lb + (1.0 - lb) * _sigmoid(f_ref[rows, :])
            logf = jnp.log(f)
            hi = logf.astype(BF16)
            r1 = logf - hi.astype(F32)
            mid = r1.astype(BF16)
            lo = (r1 - mid.astype(F32)).astype(BF16)
            expo = _dot(d_ref[...], jnp.concatenate([hi, mid, lo], axis=0))
            chunks.append((rows, 1.0 - f, expo, q_ref[rows, :].astype(F32), i_ref[rows, :],
                           g_ref[rows, :].astype(F32)))
        items = []
        for rows, kk, expo, q, v, g in chunks:
            for h in range(HGRN_HEADS):
                ls = slice(h * hd, (h + 1) * hd)
                items.append((rows, ls, h, q[:, ls], kk[:, ls], v[:, ls], g[:, ls], expo[:, ls]))
        operands = [[level_operands(qh, kh, eh, li, m) for li, m in enumerate(_HGRN_LEVELS)]
                    for _, _, _, qh, kh, _, _, eh in items]
        q_inter = [(qh * jnp.exp(eh[0:n])).astype(BF16) for _, _, _, qh, _, _, _, eh in items]
        k_state = [(kh * jnp.exp(eh[n:2 * n])).astype(BF16) for _, _, _, _, kh, _, _, eh in items]
        scores = []
        for ops in operands:
            sc = jnp.zeros((n, n), F32)
            for li, (qs, ks) in enumerate(ops):
                sc = sc + m_ref[li] * _dot_nt(qs, ks)
            scores.append(sc.astype(BF16))
        intra = [_dot(sc, it[5]) for sc, it in zip(scores, items)]
        kv = [_dot_tn(it[5], ks) for ks, it in zip(k_state, items)]
        inter = []
        for idx, (_, _, h, _, _, _, _, eh) in enumerate(items):
            st = st_ref[h]
            inter.append(_dot_nt(q_inter[idx], st.astype(BF16)))
            st_ref[h] = st * jnp.exp(eh[n - 1:n, :]) + kv[idx]
        for idx, (rows, ls, _, qh, kh, vh, gh, _) in enumerate(items):
            o = inter[idx] + intra[idx]
            o = o + jnp.sum(qh * kh, axis=-1, keepdims=True) * vh.astype(F32)
            on = _rms(o, og_ref[...])
            o_ref[rows, ls] = (on * (gh * _sigmoid(gh))).astype(o_ref.dtype)
        return carry

    lax.fori_loop(0, n_chunks // HGRN_GROUP, group, 0)


def _hgrn2(p, f_pre, lb, onorm_g, batch, seq, tt=512):
    c = HGRN_DIM
    d3, masks, _ = _hgrn_tables()
    p3 = p.reshape(batch, seq, p.shape[1])
    f3 = f_pre.reshape(batch, seq, c)
    first = 2 * CONV_DIM // c
    col = lambda j: pl.BlockSpec((None, tt, c), lambda b, t: (b, t, j))
    out = pl.pallas_call(
        functools.partial(_hgrn_body, n_chunks=tt // HGRN_CHUNK),
        grid=(batch, seq // tt),
        in_specs=[col(first), col(0), col(first + 1), col(first + 2),
                  _const_spec((1, c)), _const_spec((1, HGRN_HEAD_DIM)),
                  _const_spec(d3.shape), _const_spec(masks.shape)],
        out_specs=pl.BlockSpec((None, tt, c), lambda b, t: (b, t, 0)),
        out_shape=jax.ShapeDtypeStruct((batch, seq, c), BF16),
        scratch_shapes=[pltpu.VMEM((HGRN_HEADS, HGRN_HEAD_DIM, HGRN_HEAD_DIM), F32)],
        compiler_params=_cparams(("parallel", "arbitrary")),
        name="hgrn2",
    )(p3, f3, p3, p3, lb.reshape(1, c), onorm_g.reshape(1, HGRN_HEAD_DIM),
      jnp.asarray(d3, BF16), jnp.asarray(masks, F32))
    return out.reshape(batch * seq, c)


def _sgu_body(z_ref, lg_ref, lb_ref, w_ref, bias_ref, o_ref, *, tt):
    c = SGU_DIM
    gw = c // SGU_GROUPS
    z = z_ref[...].astype(F32)
    z = 0.5 * z * (1.0 + lax.erf(z * np.float32(1.0 / np.sqrt(2.0))))
    u = z[:, :c]
    v = _layernorm(z[:, c:], lg_ref[...], lb_ref[...]).astype(BF16)
    for ci in range(tt // SGU_CHUNK):
        rs = slice(ci * SGU_CHUNK, (ci + 1) * SGU_CHUNK)
        for gi in range(SGU_GROUPS):
            ls = slice(gi * gw, (gi + 1) * gw)
            mixed = _dot(w_ref[gi], v[rs, ls]) + bias_ref[:, ls]
            o_ref[rs, ls] = (u[rs, ls] * mixed).astype(o_ref.dtype)


def _chunked_sgu(p, ln_g, ln_b, w_s, b_s, tt=512):
    m = p.shape[0]
    c = SGU_DIM
    tril = np.tril(np.ones((SGU_CHUNK, SGU_CHUNK), dtype=bool))
    w = jnp.where(tril[None], w_s, 0.0).astype(BF16)
    bias = jnp.repeat(b_s.T, c // SGU_GROUPS, axis=1)
    return pl.pallas_call(
        functools.partial(_sgu_body, tt=tt),
        grid=(m // tt,),
        in_specs=[pl.BlockSpec((tt, 2 * c), lambda i: (i, 0)),
                  _const_spec((1, c)), _const_spec((1, c)),
                  _const_spec(w.shape), _const_spec((SGU_CHUNK, c))],
        out_specs=pl.BlockSpec((tt, c), lambda i: (i, 0)),
        out_shape=jax.ShapeDtypeStruct((m, c), BF16),
        compiler_params=_cparams(("parallel",)),
        name="chunked_sgu",
    )(p, ln_g.reshape(1, c), ln_b.reshape(1, c), w, bias)


def _rope_tables(seq):
    half = ROT_DIM // 2
    inv_freq = jnp.power(ROPE_THETA, -jnp.arange(half, dtype=F32) / half)
    ang = jnp.arange(seq, dtype=F32)[:, None] * inv_freq[None, :]
    cos, sin = jnp.cos(ang), jnp.sin(ang)
    ones = jnp.ones((seq, ATTN_HEAD_DIM - ROT_DIM), F32)
    zeros = jnp.zeros((seq, ATTN_HEAD_DIM - ROT_DIM), F32)
    zh = jnp.zeros((seq, half), F32)
    c_head = jnp.concatenate([cos, cos, ones], axis=1)
    s_dn_head = jnp.concatenate([zh, sin, zeros], axis=1)
    s_up_head = jnp.concatenate([-sin, zh, zeros], axis=1)
    tile = lambda a: jnp.concatenate([a, a], axis=1)
    return tile(c_head), tile(s_dn_head), tile(s_up_head)


def _attn_geometry(dilation):
    per_class = ATTN_SUPER // dilation
    rows = min(per_class, ATTN_BLOCK)
    return per_class, rows, per_class // rows


def _attn_prep(x_ref, g_ref, c_ref, sd_ref, su_ref, bd_ref, dst_ref):
    half = ROT_DIM // 2
    for pair in range(ATTN_CFG_DIM // LANES):
        ls = slice(pair * LANES, (pair + 1) * LANES)
        for r0 in range(0, ATTN_SUPER, ATTN_PREP_ROWS):
            rr = slice(r0, r0 + ATTN_PREP_ROWS)
            x = x_ref[rr, ls].astype(F32)
            ms = _dot((x * x).astype(BF16), bd_ref[...])
            y = x * lax.rsqrt(ms + EPS) * g_ref[:, ls]
            dst_ref[pair, rr, :] = (y * c_ref[rr, :] + pltpu.roll(y, half, 1) * sd_ref[rr, :]
                                    + pltpu.roll(y, LANES - half, 1) * su_ref[rr, :])


def _attn_band(rows):
    blk = ATTN_BLOCK
    qi = lax.broadcasted_iota(jnp.int32, (2 * rows, blk + rows), 0) % rows
    key_col = lax.broadcasted_iota(jnp.int32, (2 * rows, blk + rows), 1)
    dist = qi - (key_col - blk)
    band_bias = jnp.where((dist >= 0) & (dist <= ATTN_SPAN), 0.0, NEG_BIG).astype(F32)
    return band_bias, key_col


def _attn_units(dilation, units, step, band_bias, key_col, qd_ref, kd_ref, vd_ref, od_ref, ld_ref):
    per_class, rows, _ = _attn_geometry(dilation)
    blk = ATTN_BLOCK
    head0 = lax.broadcasted_iota(jnp.int32, (1, LANES), 1) < ATTN_HEAD_DIM
    items = []
    for r, j in units:
        q_rows = pl.ds(pl.multiple_of(r * per_class + j * blk, rows), rows)
        k_rows = pl.ds(pl.multiple_of(j * blk, blk), blk + rows)
        first_key = jnp.maximum(blk - (step * per_class + j * blk), 0)
        bias = jnp.where(key_col >= first_key, band_bias, NEG_BIG)
        for pair in range(ATTN_CFG_DIM // LANES):
            items.append((pair, r, q_rows, k_rows, bias))
    scores = []
    for pair, r, q_rows, k_rows, bias in items:
        qp = qd_ref[pair, q_rows, :]
        zero = jnp.zeros_like(qp)
        q2 = jnp.concatenate([jnp.where(head0, qp, zero), jnp.where(head0, zero, qp)], axis=0)
        scores.append(_dot_nt(q2, kd_ref[pair, r, k_rows, :]) + bias)
    probs = []
    for s in scores:
        mx = jnp.max(s, axis=-1, keepdims=True)
        p = jnp.exp2(s - mx)
        den = jnp.sum(p, axis=-1, keepdims=True)
        probs.append((p.astype(BF16), 1.0 / den, mx + jnp.log2(den)))
    outs = [_dot(p, vd_ref[pair, r, k_rows, :]) * inv
            for (p, inv, _), (pair, r, _, k_rows, _) in zip(probs, items)]
    for o2, (_, _, lse), (pair, _, q_rows, _, _) in zip(outs, probs, items):
        od_ref[pair, q_rows, :] = jnp.where(head0, o2[:rows], o2[rows:])
        ld_ref[pair, q_rows, :] = jnp.where(head0, lse[:rows], lse[rows:])


def _attn_config(cfg, dilation, step, ks_ref, v_ref, vs_ref, qs_ref, qd_ref, od_ref, ld_ref,
                 kd_ref, vd_ref, oacc_ref, lacc_ref):
    per_class, rows, n_blk = _attn_geometry(dilation)
    blk = ATTN_BLOCK
    pairs = ATTN_CFG_DIM // LANES

    @pl.when(step == 0)
    def _():
        zeros = jnp.zeros((pairs, dilation, blk, LANES), BF16)
        kd_ref[:, :, 0:blk, :] = zeros
        vd_ref[:, :, 0:blk, :] = zeros

    cur = slice(blk, blk + per_class)
    for pair in range(pairs):
        ls = slice(pair * LANES, (pair + 1) * LANES)
        if dilation == 1:
            qd_ref[pair] = qs_ref[pair].astype(BF16)
            kd_ref[pair, 0, cur, :] = ks_ref[pair].astype(BF16)
            vd_ref[pair, 0, cur, :] = v_ref[:, ls]
        else:
            vs_ref[pair] = v_ref[:, ls].astype(F32)
            for r in range(dilation):
                src = pl.ds(r, per_class, stride=dilation)
                qd_ref[pair, r * per_class:(r + 1) * per_class, :] = (
                    qs_ref[pair, src, :].astype(BF16))
                kd_ref[pair, r, cur, :] = ks_ref[pair, src, :].astype(BF16)
                vd_ref[pair, r, cur, :] = vs_ref[pair, src, :].astype(BF16)

    band_bias, key_col = _attn_band(rows)

    def group(gi, carry):
        units = [((gi * ATTN_GROUP + i) % dilation, (gi * ATTN_GROUP + i) // dilation)
                 for i in range(ATTN_GROUP)]
        _attn_units(dilation, units, step, band_bias, key_col,
                    qd_ref, kd_ref, vd_ref, od_ref, ld_ref)
        return carry

    lax.fori_loop(0, dilation * n_blk // ATTN_GROUP, group, 0)

    for pair in range(pairs):
        for r in range(dilation):
            src = slice(r * per_class, (r + 1) * per_class)
            dst = pl.ds(r, per_class, stride=dilation) if dilation > 1 else slice(0, per_class)
            oacc_ref[cfg, pair, dst, :] = od_ref[pair, src, :]
            lacc_ref[cfg, pair, dst, :] = ld_ref[pair, src, :]

    for ref in (kd_ref, vd_ref):
        ref[:, :, 0:blk, :] = ref[:, :, per_class:per_class + blk, :]


def _dil_attn_body(q_ref, k_ref, v_ref, qg_ref, kg_ref, c_ref, sd_ref, su_ref, bd_ref,
                   o_ref, qs_ref, ks_ref, vs_ref, od_ref, ld_ref, qd_ref,
                   kd0, kd1, kd2, vd0, vd1, vd2, oacc_ref, lacc_ref):
    step = pl.program_id(1)
    cfg_id = pl.program_id(2)
    n_cfg = len(DIL_CONFIGS)
    _attn_prep(q_ref, qg_ref, c_ref, sd_ref, su_ref, bd_ref, qs_ref)
    _attn_prep(k_ref, kg_ref, c_ref, sd_ref, su_ref, bd_ref, ks_ref)
    kds, vds = (kd0, kd1, kd2), (vd0, vd1, vd2)
    for cfg, (_, dilation) in enumerate(DIL_CONFIGS):
        @pl.when(cfg_id == cfg)
        def _(cfg=cfg, dilation=dilation):
            _attn_config(cfg, dilation, step, ks_ref, v_ref, vs_ref, qs_ref, qd_ref, od_ref,
                         ld_ref, kds[cfg], vds[cfg], oacc_ref, lacc_ref)

    @pl.when(cfg_id == n_cfg - 1)
    def _():
        for pair in range(ATTN_CFG_DIM // LANES):
            for r0 in range(0, ATTN_SUPER, ATTN_PREP_ROWS):
                rr = slice(r0, r0 + ATTN_PREP_ROWS)
                ls = [lacc_ref[c, pair, rr, :] for c in range(n_cfg)]
                mx = jnp.maximum(jnp.maximum(ls[0], ls[1]), ls[2])
                es = [jnp.exp2(l - mx) for l in ls]
                inv = 1.0 / (es[0] + es[1] + es[2])
                for c in range(n_cfg):
                    col = c * ATTN_CFG_DIM + pair * LANES
                    o_ref[rr, col:col + LANES] = (
                        oacc_ref[c, pair, rr, :] * (es[c] * inv)).astype(o_ref.dtype)


def _dilated_attention(p, qn_g, kn_g, batch, seq):
    m = p.shape[0]
    n_cfg = len(DIL_CONFIGS)
    n_sb = seq // ATTN_SUPER
    pairs = ATTN_CFG_DIM // LANES
    c_t, sd_t, su_t = _rope_tables(seq)
    head = np.arange(LANES) // ATTN_HEAD_DIM
    bd = jnp.asarray((head[:, None] == head[None, :]).astype(np.float32) / ATTN_HEAD_DIM, BF16)
    tile_g = lambda g: jnp.tile(g, ATTN_CFG_DIM // ATTN_HEAD_DIM).reshape(1, ATTN_CFG_DIM)
    qg = tile_g(qn_g) * np.float32(ATTN_HEAD_DIM ** -0.5 * np.log2(np.e))
    q0 = 2 * SGU_DIM // ATTN_CFG_DIM
    k0 = q0 + n_cfg
    v0 = k0 + n_cfg
    wide = lambda c0: pl.BlockSpec((ATTN_SUPER, ATTN_CFG_DIM),
                                   lambda b, s, c: (b * n_sb + s, c0 + c))
    tab = pl.BlockSpec((ATTN_SUPER, LANES), lambda b, s, c: (s, 0))
    hist = lambda d: pltpu.VMEM((pairs, d, ATTN_BLOCK + ATTN_SUPER // d, LANES), BF16)
    dils = [d for _, d in DIL_CONFIGS]
    return pl.pallas_call(
        _dil_attn_body,
        grid=(batch, n_sb, n_cfg),
        in_specs=[wide(q0), wide(k0), wide(v0),
                  _const_spec((1, ATTN_CFG_DIM)), _const_spec((1, ATTN_CFG_DIM)),
                  tab, tab, tab, _const_spec((LANES, LANES))],
        out_specs=pl.BlockSpec((ATTN_SUPER, ATTN_DIM), lambda b, s, c: (b * n_sb + s, 0)),
        out_shape=jax.ShapeDtypeStruct((m, ATTN_DIM), BF16),
        scratch_shapes=[pltpu.VMEM((pairs, ATTN_SUPER, LANES), F32)] * 5
        + [pltpu.VMEM((pairs, ATTN_SUPER, LANES), BF16)]
        + [hist(d) for d in dils] + [hist(d) for d in dils]
        + [pltpu.VMEM((n_cfg, pairs, ATTN_SUPER, LANES), F32)] * 2,
        compiler_params=_cparams(("parallel", "arbitrary", "arbitrary")),
        name="dil_attn",
    )(p, p, p, qg, tile_g(kn_g), c_t, sd_t, su_t, bd)


def _mem_kv_body(mem_ref, g_ref, w_ref, kg_ref, k_ref, v_ref):
    d = D_MODEL
    h = _rms(mem_ref[...], g_ref[...]).astype(BF16)
    for hh in range(XATTN_HEADS):
        ls = slice(hh * XATTN_HEAD_DIM, (hh + 1) * XATTN_HEAD_DIM)
        kh = _dot(h, w_ref[:, ls])
        k_ref[:, ls] = _rms(kh, kg_ref[...]).astype(k_ref.dtype)
    v_ref[...] = _dot(h, w_ref[:, d:]).astype(v_ref.dtype)


def _mem_kv(mem, g, wkv, kn_g, batch):
    d = D_MODEL
    spec = pl.BlockSpec((None, MEM_LEN, d), lambda b: (b, 0, 0))
    return pl.pallas_call(
        _mem_kv_body,
        grid=(batch,),
        in_specs=[spec, _const_spec((1, d)), _const_spec((d, 2 * d)),
                  _const_spec((1, XATTN_HEAD_DIM))],
        out_specs=[spec, spec],
        out_shape=[jax.ShapeDtypeStruct((batch, MEM_LEN, d), BF16)] * 2,
        compiler_params=_cparams(("parallel",)),
        name="mem_kv",
    )(mem, g.reshape(1, d), wkv, kn_g.reshape(1, XATTN_HEAD_DIM))


def _xattn_body(x_ref, m1_ref, m2_ref, wo1_ref, wo2_ref, gx_ref, wq_ref, qg_ref, k_ref, v_ref,
                wo_ref, o_ref):
    x1 = x_ref[...] + _dot(m1_ref[...], wo1_ref[...]) + _dot(m2_ref[...], wo2_ref[...])
    q = _dot(_rms(x1, gx_ref[...]).astype(BF16), wq_ref[...])
    heads = []
    for hh in range(XATTN_HEADS):
        ls = slice(hh * XATTN_HEAD_DIM, (hh + 1) * XATTN_HEAD_DIM)
        qh = _rms(q[:, ls], qg_ref[...]).astype(BF16)
        s = _dot_nt(qh, k_ref[:, ls])
        p = jnp.exp(s - jnp.max(s, axis=-1, keepdims=True))
        inv = 1.0 / jnp.sum(p, axis=-1, keepdims=True)
        heads.append((_dot(p.astype(BF16), v_ref[:, ls]) * inv).astype(BF16))
    o_ref[...] = x1 + _dot(jnp.concatenate(heads, axis=1), wo_ref[...])


def _xattn(x2d, m1, m2, wo1, wo2, gx, wq, qn_g, k_mem, v_mem, wo, seq, tm=ROW_TILE):
    m, d = x2d.shape
    per_b = seq // tm
    row = lambda n: pl.BlockSpec((tm, n), lambda i: (i, 0))
    mem = pl.BlockSpec((None, MEM_LEN, d), lambda i: (i // per_b, 0, 0))
    qg = (qn_g * np.float32(XATTN_HEAD_DIM ** -0.5)).reshape(1, XATTN_HEAD_DIM)
    return pl.pallas_call(
        _xattn_body,
        grid=(m // tm,),
        in_specs=[row(d), row(m1.shape[1]), row(m2.shape[1]),
                  _const_spec(wo1.shape), _const_spec(wo2.shape), _const_spec((1, d)),
                  _const_spec(wq.shape), _const_spec((1, XATTN_HEAD_DIM)), mem, mem,
                  _const_spec(wo.shape)],
        out_specs=row(d),
        out_shape=jax.ShapeDtypeStruct((m, d), F32),
        compiler_params=_cparams(("parallel",)),
        name="xattn",
    )(x2d, m1, m2, wo1, wo2, gx.reshape(1, d), wq, qg, k_mem, v_mem, wo)


def _mlp_body(x_ref, g_ref, w1_ref, w2_ref, o_ref, *, ff_tile):
    x = x_ref[...]
    h = _rms(x, g_ref[...]).astype(BF16)
    acc = x
    for c0 in range(0, D_FF, ff_tile):
        hid = jnp.maximum(_dot(h, w1_ref[:, c0:c0 + ff_tile]), 0.0)
        acc = acc + _dot((hid * hid).astype(BF16), w2_ref[c0:c0 + ff_tile, :])
    o_ref[...] = acc


def _mlp(x2d, g, w1, w2, tm=ROW_TILE, ff_tile=1024):
    m, d = x2d.shape
    row = pl.BlockSpec((tm, d), lambda i: (i, 0))
    return pl.pallas_call(
        functools.partial(_mlp_body, ff_tile=ff_tile),
        grid=(m // tm,),
        in_specs=[row, _const_spec((1, d)), _const_spec(w1.shape), _const_spec(w2.shape)],
        out_specs=row,
        out_shape=jax.ShapeDtypeStruct((m, d), F32),
        compiler_params=_cparams(("parallel",)),
        name="mlp",
    )(x2d, g.reshape(1, d), w1, w2)


def kernel(x, mem, norm_mix_g, ev_w_in, conv_dw_w, conv_dw_b, conv_ln_g, conv_ln_b,
           hgrn_lb_logits, hgrn_onorm_g, ev_w_out, od_w_in, sgu_ln_g, sgu_ln_b, sgu_w, sgu_b,
           attn_qnorm_g, attn_knorm_g, od_w_out, norm_xattn_g, norm_mem_g, xattn_wq, xattn_wkv,
           xattn_qnorm_g, xattn_knorm_g, xattn_wo, norm_mlp_g, mlp_w1, mlp_w2):
    batch, seq, d = x.shape
    bf = lambda a: a.astype(BF16)
    lb_all = jnp.cumsum(jax.nn.softmax(hgrn_lb_logits.astype(F32), axis=0), axis=0)
    lb_all = lb_all - lb_all[0]
    xf = x.reshape(batch * seq, d)
    for l in range(DEPTH):
        if l % 2 == 0:
            e = l // 2
            w_in = bf(ev_w_in[e])
            p, f_pre = _norm_proj(xf, norm_mix_g[l], w_in, EVEN_OUTS, EVEN_PIECES)
            m1 = _conformer_conv(p, conv_dw_w[e], conv_dw_b[e], conv_ln_g[e], conv_ln_b[e],
                                 batch, seq)
            m2 = _hgrn2(p, f_pre, lb_all[e], hgrn_onorm_g[e], batch, seq)
            w_out = bf(ev_w_out[e])
            wo1, wo2 = w_out[:CONV_DIM], w_out[CONV_DIM:]
        else:
            o = l // 2
            w_in = bf(od_w_in[o])
            (p,) = _norm_proj(xf, norm_mix_g[l], w_in, ODD_OUTS, ODD_PIECES)
            m1 = _chunked_sgu(p, sgu_ln_g[o], sgu_ln_b[o], sgu_w[o], sgu_b[o])
            m2 = _dilated_attention(p, attn_qnorm_g[o], attn_knorm_g[o], batch, seq)
            w_out = bf(od_w_out[o])
            wo1, wo2 = w_out[:SGU_DIM], w_out[SGU_DIM:]
        k_mem, v_mem = _mem_kv(mem, norm_mem_g[l], bf(xattn_wkv[l]), xattn_knorm_g[l], batch)
        xf = _xattn(xf, m1, m2, wo1, wo2, norm_xattn_g[l], bf(xattn_wq[l]), xattn_qnorm_g[l],
                    k_mem, v_mem, bf(xattn_wo[l]), seq)
        xf = _mlp(xf, norm_mlp_g[l], bf(mlp_w1[l]), bf(mlp_w2[l]))
    return xf.reshape(batch, seq, d)
```

```python
import functools

import numpy as np
import jax
import jax.numpy as jnp
from jax import lax
from jax.experimental import pallas as pl
from jax.experimental.pallas import tpu as pltpu

F32 = jnp.float32
BF16 = jnp.bfloat16

D_MODEL = 1024
DEPTH = 4
EPS = 1e-6
CONV_DIM = 512
CONV_WIDTH = 31
HGRN_DIM = 512
HGRN_HEAD_DIM = 128
HGRN_HEADS = 4
HGRN_CHUNK = 64
SGU_DIM = 512
SGU_GROUPS = 4
SGU_CHUNK = 128
ATTN_HEAD_DIM = 64
DIL_CONFIGS = ((128, 1), (512, 4), (2048, 16))
ATTN_HEADS_PER_CFG = 4
ATTN_CFG_DIM = ATTN_HEADS_PER_CFG * ATTN_HEAD_DIM
ATTN_DIM = ATTN_CFG_DIM * len(DIL_CONFIGS)
ATTN_BLOCK = 128
ATTN_SPAN = 128
ROPE_THETA = 500000.0
ROT_DIM = 16
MEM_LEN = 256
XATTN_HEADS = 4
XATTN_HEAD_DIM = 256
D_FF = 4096

LANES = 128
SUBLANES = 8
ROW_TILE = 1024
CONV_HALO = 32
ATTN_SUPER = 1024
ATTN_PREP_ROWS = 256
ATTN_GROUP = 4
NEG_BIG = -1e30
VMEM_LIMIT = 56 * 1024 * 1024


def _cparams(sem):
    return pltpu.CompilerParams(dimension_semantics=sem, vmem_limit_bytes=VMEM_LIMIT)


def _const_spec(shape):
    nd = len(shape)
    return pl.BlockSpec(shape, lambda *_: (0,) * nd, pipeline_mode=pl.Buffered(1))


def _dot(a, b):
    return jnp.dot(a, b, preferred_element_type=F32)


def _dot_nt(a, b):
    return lax.dot_general(a, b, (((1,), (1,)), ((), ())), preferred_element_type=F32)


def _dot_tn(a, b):
    return lax.dot_general(a, b, (((0,), (0,)), ((), ())), preferred_element_type=F32)


def _rms(x, g):
    return x * lax.rsqrt(jnp.mean(x * x, axis=-1, keepdims=True) + EPS) * g


def _layernorm(x, g, b):
    mu = jnp.mean(x, axis=-1, keepdims=True)
    xc = x - mu
    return xc * lax.rsqrt(jnp.mean(xc * xc, axis=-1, keepdims=True) + EPS) * g + b


def _sigmoid(x):
    return 1.0 / (1.0 + jnp.exp(-x))


def _norm_proj_body(x_ref, g_ref, w_ref, *o_refs, pieces):
    h = _rms(x_ref[...], g_ref[...]).astype(BF16)
    for out, w0, o0, n in pieces:
        o_refs[out][:, o0:o0 + n] = _dot(h, w_ref[:, w0:w0 + n]).astype(o_refs[out].dtype)


def _norm_proj(x2d, g, w, outs, pieces, tm=ROW_TILE):
    m, d = x2d.shape
    return pl.pallas_call(
        functools.partial(_norm_proj_body, pieces=pieces),
        grid=(m // tm,),
        in_specs=[pl.BlockSpec((tm, d), lambda i: (i, 0)),
                  _const_spec((1, d)), _const_spec(w.shape)],
        out_specs=[pl.BlockSpec((tm, n), lambda i: (i, 0)) for n, _ in outs],
        out_shape=[jax.ShapeDtypeStruct((m, n), dt) for n, dt in outs],
        compiler_params=_cparams(("parallel",)),
        name="norm_proj",
    )(x2d, g.reshape(1, d), w)


ODD_IN = 2 * SGU_DIM + 3 * ATTN_DIM
ODD_OUTS = ((ODD_IN, BF16),)
ODD_PIECES = ((0, 0, 0, 1024), (0, 1024, 1024, 768), (0, 1792, 1792, 768), (0, 2560, 2560, 768))


CONV_ROWS = 64
EVEN_ROW_TILE = 512


def _even_proj_body(x_ref, g_ref, w_ref, cw_ref, cb_ref, lg_ref, lb_ref,
                    a_ref, qig_ref, f_ref, hbuf, sh_ref, *, tm):
    c = CONV_DIM
    hd = HGRN_DIM
    n_rows = CONV_HALO + tm
    h = _rms(x_ref[...], g_ref[...]).astype(BF16)
    a_in = _dot(h, w_ref[:, 0:2 * c])
    c0 = 2 * c
    qig_ref[:, 0:hd] = _dot(h, w_ref[:, c0:c0 + hd]).astype(BF16)
    f_ref[...] = _dot(h, w_ref[:, c0 + hd:c0 + 2 * hd])
    qig_ref[:, hd:3 * hd] = _dot(h, w_ref[:, c0 + 2 * hd:c0 + 4 * hd]).astype(BF16)

    @pl.when(pl.program_id(1) == 0)
    def _():
        hbuf[0:CONV_HALO, :] = jnp.zeros((CONV_HALO, c), F32)

    hbuf[CONV_HALO:n_rows, :] = a_in[:, :c] * _sigmoid(a_in[:, c:])
    for s in range(1, SUBLANES):
        sh_ref[s, 0:n_rows - SUBLANES, :] = hbuf[s:s + n_rows - SUBLANES, :]
    base = CONV_HALO - (CONV_WIDTH - 1)
    for r0 in range(0, tm, CONV_ROWS):
        acc = jnp.zeros((CONV_ROWS, c), F32) + cb_ref[...]
        for j in range(CONV_WIDTH):
            s = (base + j) % SUBLANES
            a0 = base + j - s + r0
            tap = hbuf[a0:a0 + CONV_ROWS, :] if s == 0 else sh_ref[s, a0:a0 + CONV_ROWS, :]
            acc = acc + cw_ref[j:j + 1, :] * tap
        y = _layernorm(acc, lg_ref[...], lb_ref[...])
        a_ref[r0:r0 + CONV_ROWS, :] = (y * _sigmoid(y)).astype(a_ref.dtype)
    hbuf[0:CONV_HALO, :] = hbuf[tm:n_rows, :]


def _even_proj(x2d, g, w, dw_w, dw_b, ln_g, ln_b, batch, seq, tm=EVEN_ROW_TILE):
    m, d = x2d.shape
    c = CONV_DIM
    hd = HGRN_DIM
    w_pad = jnp.zeros((CONV_HALO, c), F32).at[:CONV_WIDTH].set(dw_w)
    per_b = seq // tm
    row = lambda n: pl.BlockSpec((tm, n), lambda b, t: (b * per_b + t, 0))
    return pl.pallas_call(
        functools.partial(_even_proj_body, tm=tm),
        grid=(batch, per_b),
        in_specs=[row(d), _const_spec((1, d)), _const_spec(w.shape),
                  _const_spec((CONV_HALO, c)), _const_spec((1, c)),
                  _const_spec((1, c)), _const_spec((1, c))],
        out_specs=[row(c), row(3 * hd), row(hd)],
        out_shape=[jax.ShapeDtypeStruct((m, c), BF16), jax.ShapeDtypeStruct((m, 3 * hd), BF16),
                   jax.ShapeDtypeStruct((m, hd), F32)],
        scratch_shapes=[pltpu.VMEM((CONV_HALO + tm, c), F32),
                        pltpu.VMEM((SUBLANES, CONV_HALO + tm, c), F32)],
        compiler_params=_cparams(("parallel", "arbitrary")),
        name="even_proj",
    )(x2d, g.reshape(1, d), w, w_pad, dw_b.reshape(1, c), ln_g.reshape(1, c), ln_b.reshape(1, c))


_HGRN_LEVELS = (32, 16, 8, 4, 2, 1)
_HGRN_MATRIX_LEVELS = (4, 2)
HGRN_GROUP = 2


def _hgrn_tables():
    n = HGRN_CHUNK
    t = np.arange(n)[:, None]
    u = np.arange(n)[None, :]
    blocks = [u <= t, u > t]
    masks = []
    offsets = {}
    row = 2 * n
    for m in _HGRN_LEVELS:
        c = (t // (2 * m)) * (2 * m)
        upper = (t - c) >= m
        if m in _HGRN_MATRIX_LEVELS:
            blocks.append(upper & (u >= c + m) & (u <= t))
            blocks.append((~upper) & (u >= t + 1) & (u <= c + m - 1))
            offsets[m] = (row, row + n)
            row += 2 * n
        cs = (u // (2 * m)) * (2 * m)
        masks.append(((t // (2 * m)) == (u // (2 * m))) & upper & ((u - cs) < m))
    d = np.concatenate(blocks, axis=0).astype(np.float32)
    return np.concatenate([d, d, d], axis=1), np.stack(masks).astype(np.float32), offsets


_HGRN_OFFSETS = _hgrn_tables()[2]


def _hgrn_body(q_ref, f_ref, i_ref, g_ref, lb_ref, og_ref, d_ref, m_ref, o_ref, st_ref,
               *, n_chunks):
    n = HGRN_CHUNK
    hd = HGRN_HEAD_DIM

    @pl.when(pl.program_id(1) == 0)
    def _():
        st_ref[...] = jnp.zeros(st_ref.shape, F32)

    lb = lb_ref[...]
    half = n // 2

    def level_operands(qh, kh, eh, li, m):
        if m == 1:
            return (qh * (1.0 - kh)).astype(BF16), kh.astype(BF16)
        if m in _HGRN_MATRIX_LEVELS:
            q0, k0 = _HGRN_OFFSETS[m]
            return ((qh * jnp.exp2(eh[q0:q0 + n])).astype(BF16),
                    (kh * jnp.exp2(eh[k0:k0 + n])).astype(BF16))
        e_b = eh[0:n]
        q_parts, k_parts = [], []
        for b in range(half // m):
            lo = slice(2 * b * m, (2 * b + 1) * m)
            up = slice((2 * b + 1) * m, (2 * b + 2) * m)
            ref = e_b[(2 * b + 1) * m - 1:(2 * b + 1) * m, :]
            q_parts += [qh[lo], qh[up] * jnp.exp2(e_b[up] - ref)]
            k_parts += [kh[lo] * jnp.exp2(ref - e_b[lo]), kh[up]]
        return (jnp.concatenate(q_parts, axis=0).astype(BF16),
                jnp.concatenate(k_parts, axis=0).astype(BF16))

    def group(gi, carry):
        chunks = []
        for ci in range(HGRN_GROUP):
            rows = pl.ds(pl.multiple_of((gi * HGRN_GROUP + ci) * n, n), n)
            f = lb + (1.0 - lb) * _sigmoid(f_ref[rows, :])
            logf = jnp.log2(f)
            hi = logf.astype(BF16)
            r1 = logf - hi.astype(F32)
            mid = r1.astype(BF16)
            lo = (r1 - mid.astype(F32)).astype(BF16)
            expo = _dot(d_ref[...], jnp.concatenate([hi, mid, lo], axis=0))
            chunks.append((rows, 1.0 - f, expo, q_ref[rows, :].astype(F32), i_ref[rows, :],
                           g_ref[rows, :].astype(F32)))
        items = []
        for rows, kk, expo, q, v, g in chunks:
            for h in range(HGRN_HEADS):
                ls = slice(h * hd, (h + 1) * hd)
                items.append((rows, ls, h, q[:, ls], kk[:, ls], v[:, ls], g[:, ls], expo[:, ls]))
        operands = [[level_operands(qh, kh, eh, li, m) for li, m in enumerate(_HGRN_LEVELS)]
                    for _, _, _, qh, kh, _, _, eh in items]
        q_inter = [(qh * jnp.exp2(eh[0:n])).astype(BF16) for _, _, _, qh, _, _, _, eh in items]
        k_state = [(kh * jnp.exp2(eh[n:2 * n])).astype(BF16) for _, _, _, _, kh, _, _, eh in items]
        scores = []
        for ops in operands:
            sc = jnp.zeros((n, n), F32)
            for li, (qs, ks) in enumerate(ops):
                sc = sc + m_ref[li] * _dot_nt(qs, ks)
            scores.append(sc.astype(BF16))
        intra = [_dot(sc, it[5]) for sc, it in zip(scores, items)]
        kv = [_dot_tn(it[5], ks) for ks, it in zip(k_state, items)]
        inter = []
        for idx, (_, _, h, _, _, _, _, eh) in enumerate(items):
            st = st_ref[h]
            inter.append(_dot_nt(q_inter[idx], st.astype(BF16)))
            st_ref[h] = st * jnp.exp2(eh[n - 1:n, :]) + kv[idx]
        for idx, (rows, ls, _, qh, kh, vh, gh, _) in enumerate(items):
            o = inter[idx] + intra[idx]
            o = o + jnp.sum(qh * kh, axis=-1, keepdims=True) * vh.astype(F32)
            on = _rms(o, og_ref[...])
            o_ref[rows, ls] = (on * (gh * _sigmoid(gh))).astype(o_ref.dtype)
        return carry

    lax.fori_loop(0, n_chunks // HGRN_GROUP, group, 0)


def _hgrn2(qig, f_pre, lb, onorm_g, batch, seq, tt=512):
    c = HGRN_DIM
    d3, masks, _ = _hgrn_tables()
    p3 = qig.reshape(batch, seq, qig.shape[1])
    f3 = f_pre.reshape(batch, seq, c)
    col = lambda j: pl.BlockSpec((None, tt, c), lambda b, t: (b, t, j))
    out = pl.pallas_call(
        functools.partial(_hgrn_body, n_chunks=tt // HGRN_CHUNK),
        grid=(batch, seq // tt),
        in_specs=[col(0), col(0), col(1), col(2),
                  _const_spec((1, c)), _const_spec((1, HGRN_HEAD_DIM)),
                  _const_spec(d3.shape), _const_spec(masks.shape)],
        out_specs=pl.BlockSpec((None, tt, c), lambda b, t: (b, t, 0)),
        out_shape=jax.ShapeDtypeStruct((batch, seq, c), BF16),
        scratch_shapes=[pltpu.VMEM((HGRN_HEADS, HGRN_HEAD_DIM, HGRN_HEAD_DIM), F32)],
        compiler_params=_cparams(("parallel", "arbitrary")),
        name="hgrn2",
    )(p3, f3, p3, p3, lb.reshape(1, c), onorm_g.reshape(1, HGRN_HEAD_DIM),
      jnp.asarray(d3, BF16), jnp.asarray(masks, F32))
    return out.reshape(batch * seq, c)


def _sgu_body(z_ref, lg_ref, lb_ref, w_ref, bias_ref, o_ref, *, tt):
    c = SGU_DIM
    gw = c // SGU_GROUPS
    z = z_ref[...].astype(F32)
    z = 0.5 * z * (1.0 + lax.erf(z * np.float32(1.0 / np.sqrt(2.0))))
    u = z[:, :c]
    v = _layernorm(z[:, c:], lg_ref[...], lb_ref[...]).astype(BF16)
    for ci in range(tt // SGU_CHUNK):
        rs = slice(ci * SGU_CHUNK, (ci + 1) * SGU_CHUNK)
        for gi in range(SGU_GROUPS):
            ls = slice(gi * gw, (gi + 1) * gw)
            mixed = _dot(w_ref[gi], v[rs, ls]) + bias_ref[:, ls]
            o_ref[rs, ls] = (u[rs, ls] * mixed).astype(o_ref.dtype)


def _chunked_sgu(p, ln_g, ln_b, w_s, b_s, tt=512):
    m = p.shape[0]
    c = SGU_DIM
    tril = np.tril(np.ones((SGU_CHUNK, SGU_CHUNK), dtype=bool))
    w = jnp.where(tril[None], w_s, 0.0).astype(BF16)
    bias = jnp.repeat(b_s.T, c // SGU_GROUPS, axis=1)
    return pl.pallas_call(
        functools.partial(_sgu_body, tt=tt),
        grid=(m // tt,),
        in_specs=[pl.BlockSpec((tt, 2 * c), lambda i: (i, 0)),
                  _const_spec((1, c)), _const_spec((1, c)),
                  _const_spec(w.shape), _const_spec((SGU_CHUNK, c))],
        out_specs=pl.BlockSpec((tt, c), lambda i: (i, 0)),
        out_shape=jax.ShapeDtypeStruct((m, c), BF16),
        compiler_params=_cparams(("parallel",)),
        name="chunked_sgu",
    )(p, ln_g.reshape(1, c), ln_b.reshape(1, c), w, bias)


def _rope_tables(seq):
    half = ROT_DIM // 2
    inv_freq = jnp.power(ROPE_THETA, -jnp.arange(half, dtype=F32) / half)
    ang = jnp.arange(seq, dtype=F32)[:, None] * inv_freq[None, :]
    cos, sin = jnp.cos(ang), jnp.sin(ang)
    ones = jnp.ones((seq, ATTN_HEAD_DIM - ROT_DIM), F32)
    zeros = jnp.zeros((seq, ATTN_HEAD_DIM - ROT_DIM), F32)
    zh = jnp.zeros((seq, half), F32)
    c_head = jnp.concatenate([cos, cos, ones], axis=1)
    s_dn_head = jnp.concatenate([zh, sin, zeros], axis=1)
    s_up_head = jnp.concatenate([-sin, zh, zeros], axis=1)
    tile = lambda a: jnp.concatenate([a, a], axis=1)
    return tile(c_head), tile(s_dn_head), tile(s_up_head)


def _attn_geometry(dilation):
    per_class = ATTN_SUPER // dilation
    rows = min(per_class, ATTN_BLOCK)
    return per_class, rows, per_class // rows


def _attn_prep(x_ref, g_ref, c_ref, sd_ref, su_ref, bd_ref, dst_ref):
    half = ROT_DIM // 2
    for pair in range(ATTN_CFG_DIM // LANES):
        ls = slice(pair * LANES, (pair + 1) * LANES)
        for r0 in range(0, ATTN_SUPER, ATTN_PREP_ROWS):
            rr = slice(r0, r0 + ATTN_PREP_ROWS)
            x = x_ref[rr, ls].astype(F32)
            ms = _dot((x * x).astype(BF16), bd_ref[...])
            y = x * lax.rsqrt(ms + EPS) * g_ref[:, ls]
            dst_ref[pair, rr, :] = (y * c_ref[rr, :] + pltpu.roll(y, half, 1) * sd_ref[rr, :]
                                    + pltpu.roll(y, LANES - half, 1) * su_ref[rr, :])


def _attn_band(rows):
    blk = ATTN_BLOCK
    qi = lax.broadcasted_iota(jnp.int32, (2 * rows, blk + rows), 0) % rows
    key_col = lax.broadcasted_iota(jnp.int32, (2 * rows, blk + rows), 1)
    dist = qi - (key_col - blk)
    band_bias = jnp.where((dist >= 0) & (dist <= ATTN_SPAN), 0.0, NEG_BIG).astype(F32)
    return band_bias, key_col


def _attn_units(dilation, units, step, band_bias, key_col, qd_ref, kd_ref, vd_ref, od_ref, ld_ref):
    per_class, rows, _ = _attn_geometry(dilation)
    blk = ATTN_BLOCK
    head0 = lax.broadcasted_iota(jnp.int32, (1, LANES), 1) < ATTN_HEAD_DIM
    items = []
    for r, j in units:
        q_rows = pl.ds(pl.multiple_of(r * per_class + j * blk, rows), rows)
        k_rows = pl.ds(pl.multiple_of(j * blk, blk), blk + rows)
        first_key = jnp.maximum(blk - (step * per_class + j * blk), 0)
        bias = jnp.where(key_col >= first_key, band_bias, NEG_BIG)
        for pair in range(ATTN_CFG_DIM // LANES):
            items.append((pair, r, q_rows, k_rows, bias))
    scores = []
    for pair, r, q_rows, k_rows, bias in items:
        qp = qd_ref[pair, q_rows, :]
        zero = jnp.zeros_like(qp)
        q2 = jnp.concatenate([jnp.where(head0, qp, zero), jnp.where(head0, zero, qp)], axis=0)
        scores.append(_dot_nt(q2, kd_ref[pair, r, k_rows, :]) + bias)
    probs = []
    for s in scores:
        mx = jnp.max(s, axis=-1, keepdims=True)
        p = jnp.exp2(s - mx)
        den = jnp.sum(p, axis=-1, keepdims=True)
        probs.append((p.astype(BF16), 1.0 / den, mx + jnp.log2(den)))
    outs = [_dot(p, vd_ref[pair, r, k_rows, :]) * inv
            for (p, inv, _), (pair, r, _, k_rows, _) in zip(probs, items)]
    for o2, (_, _, lse), (pair, _, q_rows, _, _) in zip(outs, probs, items):
        od_ref[pair, q_rows, :] = jnp.where(head0, o2[:rows], o2[rows:])
        ld_ref[pair, q_rows, :] = jnp.where(head0, lse[:rows], lse[rows:])


def _attn_config(cfg, dilation, step, ks_ref, v_ref, vs_ref, qs_ref, qd_ref, od_ref, ld_ref,
                 kd_ref, vd_ref, oacc_ref, lacc_ref):
    per_class, rows, n_blk = _attn_geometry(dilation)
    blk = ATTN_BLOCK
    pairs = ATTN_CFG_DIM // LANES

    @pl.when(step == 0)
    def _():
        zeros = jnp.zeros((pairs, dilation, blk, LANES), BF16)
        kd_ref[:, :, 0:blk, :] = zeros
        vd_ref[:, :, 0:blk, :] = zeros

    cur = slice(blk, blk + per_class)
    for pair in range(pairs):
        ls = slice(pair * LANES, (pair + 1) * LANES)
        if dilation == 1:
            qd_ref[pair] = qs_ref[pair].astype(BF16)
            kd_ref[pair, 0, cur, :] = ks_ref[pair].astype(BF16)
            vd_ref[pair, 0, cur, :] = v_ref[:, ls]
        else:
            vs_ref[pair] = v_ref[:, ls].astype(F32)
            for r in range(dilation):
                src = pl.ds(r, per_class, stride=dilation)
                qd_ref[pair, r * per_class:(r + 1) * per_class, :] = (
                    qs_ref[pair, src, :].astype(BF16))
                kd_ref[pair, r, cur, :] = ks_ref[pair, src, :].astype(BF16)
                vd_ref[pair, r, cur, :] = vs_ref[pair, src, :].astype(BF16)

    band_bias, key_col = _attn_band(rows)

    def group(gi, carry):
        units = [((gi * ATTN_GROUP + i) % dilation, (gi * ATTN_GROUP + i) // dilation)
                 for i in range(ATTN_GROUP)]
        _attn_units(dilation, units, step, band_bias, key_col,
                    qd_ref, kd_ref, vd_ref, od_ref, ld_ref)
        return carry

    lax.fori_loop(0, dilation * n_blk // ATTN_GROUP, group, 0)

    for pair in range(pairs):
        for r in range(dilation):
            src = slice(r * per_class, (r + 1) * per_class)
            dst = pl.ds(r, per_class, stride=dilation) if dilation > 1 else slice(0, per_class)
            oacc_ref[cfg, pair, dst, :] = od_ref[pair, src, :]
            lacc_ref[cfg, pair, dst, :] = ld_ref[pair, src, :]

    for ref in (kd_ref, vd_ref):
        ref[:, :, 0:blk, :] = ref[:, :, per_class:per_class + blk, :]


def _dil_attn_body(q_ref, k_ref, v_ref, qg_ref, kg_ref, c_ref, sd_ref, su_ref, bd_ref,
                   o_ref, qs_ref, ks_ref, vs_ref, od_ref, ld_ref, qd_ref,
                   kd0, kd1, kd2, vd0, vd1, vd2, oacc_ref, lacc_ref):
    step = pl.program_id(1)
    cfg_id = pl.program_id(2)
    n_cfg = len(DIL_CONFIGS)
    _attn_prep(q_ref, qg_ref, c_ref, sd_ref, su_ref, bd_ref, qs_ref)
    _attn_prep(k_ref, kg_ref, c_ref, sd_ref, su_ref, bd_ref, ks_ref)
    kds, vds = (kd0, kd1, kd2), (vd0, vd1, vd2)
    for cfg, (_, dilation) in enumerate(DIL_CONFIGS):
        @pl.when(cfg_id == cfg)
        def _(cfg=cfg, dilation=dilation):
            _attn_config(cfg, dilation, step, ks_ref, v_ref, vs_ref, qs_ref, qd_ref, od_ref,
                         ld_ref, kds[cfg], vds[cfg], oacc_ref, lacc_ref)

    @pl.when(cfg_id == n_cfg - 1)
    def _():
        for pair in range(ATTN_CFG_DIM // LANES):
            for r0 in range(0, ATTN_SUPER, ATTN_PREP_ROWS):
                rr = slice(r0, r0 + ATTN_PREP_ROWS)
                ls = [lacc_ref[c, pair, rr, :] for c in range(n_cfg)]
                mx = jnp.maximum(jnp.maximum(ls[0], ls[1]), ls[2])
                es = [jnp.exp2(l - mx) for l in ls]
                inv = 1.0 / (es[0] + es[1] + es[2])
                for c in range(n_cfg):
                    col = c * ATTN_CFG_DIM + pair * LANES
                    o_ref[rr, col:col + LANES] = (
                        oacc_ref[c, pair, rr, :] * (es[c] * inv)).astype(o_ref.dtype)


def _dilated_attention(p, qn_g, kn_g, batch, seq):
    m = p.shape[0]
    n_cfg = len(DIL_CONFIGS)
    n_sb = seq // ATTN_SUPER
    pairs = ATTN_CFG_DIM // LANES
    c_t, sd_t, su_t = _rope_tables(seq)
    head = np.arange(LANES) // ATTN_HEAD_DIM
    bd = jnp.asarray((head[:, None] == head[None, :]).astype(np.float32) / ATTN_HEAD_DIM, BF16)
    tile_g = lambda g: jnp.tile(g, ATTN_CFG_DIM // ATTN_HEAD_DIM).reshape(1, ATTN_CFG_DIM)
    qg = tile_g(qn_g) * np.float32(ATTN_HEAD_DIM ** -0.5 * np.log2(np.e))
    q0 = 2 * SGU_DIM // ATTN_CFG_DIM
    k0 = q0 + n_cfg
    v0 = k0 + n_cfg
    wide = lambda c0: pl.BlockSpec((ATTN_SUPER, ATTN_CFG_DIM),
                                   lambda b, s, c: (b * n_sb + s, c0 + c))
    tab = pl.BlockSpec((ATTN_SUPER, LANES), lambda b, s, c: (s, 0))
    hist = lambda d: pltpu.VMEM((pairs, d, ATTN_BLOCK + ATTN_SUPER // d, LANES), BF16)
    dils = [d for _, d in DIL_CONFIGS]
    return pl.pallas_call(
        _dil_attn_body,
        grid=(batch, n_sb, n_cfg),
        in_specs=[wide(q0), wide(k0), wide(v0),
                  _const_spec((1, ATTN_CFG_DIM)), _const_spec((1, ATTN_CFG_DIM)),
                  tab, tab, tab, _const_spec((LANES, LANES))],
        out_specs=pl.BlockSpec((ATTN_SUPER, ATTN_DIM), lambda b, s, c: (b * n_sb + s, 0)),
        out_shape=jax.ShapeDtypeStruct((m, ATTN_DIM), BF16),
        scratch_shapes=[pltpu.VMEM((pairs, ATTN_SUPER, LANES), F32)] * 5
        + [pltpu.VMEM((pairs, ATTN_SUPER, LANES), BF16)]
        + [hist(d) for d in dils] + [hist(d) for d in dils]
        + [pltpu.VMEM((n_cfg, pairs, ATTN_SUPER, LANES), F32)] * 2,
        compiler_params=_cparams(("parallel", "arbitrary", "arbitrary")),
        name="dil_attn",
    )(p, p, p, qg, tile_g(kn_g), c_t, sd_t, su_t, bd)


def _mem_kv_body(mem_ref, g_ref, w_ref, kg_ref, k_ref, v_ref):
    d = D_MODEL
    h = _rms(mem_ref[...], g_ref[...]).astype(BF16)
    for hh in range(XATTN_HEADS):
        ls = slice(hh * XATTN_HEAD_DIM, (hh + 1) * XATTN_HEAD_DIM)
        kh = _dot(h, w_ref[:, ls])
        k_ref[:, ls] = _rms(kh, kg_ref[...]).astype(k_ref.dtype)
    v_ref[...] = _dot(h, w_ref[:, d:]).astype(v_ref.dtype)


def _mem_kv(mem, g, wkv, kn_g, batch):
    d = D_MODEL
    spec = pl.BlockSpec((None, MEM_LEN, d), lambda b: (b, 0, 0))
    return pl.pallas_call(
        _mem_kv_body,
        grid=(batch,),
        in_specs=[spec, _const_spec((1, d)), _const_spec((d, 2 * d)),
                  _const_spec((1, XATTN_HEAD_DIM))],
        out_specs=[spec, spec],
        out_shape=[jax.ShapeDtypeStruct((batch, MEM_LEN, d), BF16)] * 2,
        compiler_params=_cparams(("parallel",)),
        name="mem_kv",
    )(mem, g.reshape(1, d), wkv, kn_g.reshape(1, XATTN_HEAD_DIM))


def _xattn_body(x_ref, m1_ref, m2_ref, wo1_ref, wo2_ref, gx_ref, wq_ref, qg_ref, k_ref, v_ref,
                wo_ref, o_ref):
    x1 = x_ref[...] + _dot(m1_ref[...], wo1_ref[...]) + _dot(m2_ref[...], wo2_ref[...])
    q = _dot(_rms(x1, gx_ref[...]).astype(BF16), wq_ref[...])
    heads = []
    for hh in range(XATTN_HEADS):
        ls = slice(hh * XATTN_HEAD_DIM, (hh + 1) * XATTN_HEAD_DIM)
        qh = _rms(q[:, ls], qg_ref[...]).astype(BF16)
        s = _dot_nt(qh, k_ref[:, ls])
        p = jnp.exp(s - jnp.max(s, axis=-1, keepdims=True))
        inv = 1.0 / jnp.sum(p, axis=-1, keepdims=True)
        heads.append((_dot(p.astype(BF16), v_ref[:, ls]) * inv).astype(BF16))
    o_ref[...] = x1 + _dot(jnp.concatenate(heads, axis=1), wo_ref[...])


def _xattn(x2d, m1, m2, wo1, wo2, gx, wq, qn_g, k_mem, v_mem, wo, seq, tm=ROW_TILE):
    m, d = x2d.shape
    per_b = seq // tm
    row = lambda n: pl.BlockSpec((tm, n), lambda i: (i, 0))
    mem = pl.BlockSpec((None, MEM_LEN, d), lambda i: (i // per_b, 0, 0))
    qg = (qn_g * np.float32(XATTN_HEAD_DIM ** -0.5)).reshape(1, XATTN_HEAD_DIM)
    return pl.pallas_call(
        _xattn_body,
        grid=(m // tm,),
        in_specs=[row(d), row(m1.shape[1]), row(m2.shape[1]),
                  _const_spec(wo1.shape), _const_spec(wo2.shape), _const_spec((1, d)),
                  _const_spec(wq.shape), _const_spec((1, XATTN_HEAD_DIM)), mem, mem,
                  _const_spec(wo.shape)],
        out_specs=row(d),
        out_shape=jax.ShapeDtypeStruct((m, d), F32),
        compiler_params=_cparams(("parallel",)),
        name="xattn",
    )(x2d, m1, m2, wo1, wo2, gx.reshape(1, d), wq, qg, k_mem, v_mem, wo)


def _mlp_body(x_ref, g_ref, w1_ref, w2_ref, o_ref, *, ff_tile):
    x = x_ref[...]
    h = _rms(x, g_ref[...]).astype(BF16)
    acc = x
    for c0 in range(0, D_FF, ff_tile):
        hid = jnp.maximum(_dot(h, w1_ref[:, c0:c0 + ff_tile]), 0.0)
        acc = acc + _dot((hid * hid).astype(BF16), w2_ref[c0:c0 + ff_tile, :])
    o_ref[...] = acc


def _mlp(x2d, g, w1, w2, tm=ROW_TILE, ff_tile=1024):
    m, d = x2d.shape
    row = pl.BlockSpec((tm, d), lambda i: (i, 0))
    return pl.pallas_call(
        functools.partial(_mlp_body, ff_tile=ff_tile),
        grid=(m // tm,),
        in_specs=[row, _const_spec((1, d)), _const_spec(w1.shape), _const_spec(w2.shape)],
        out_specs=row,
        out_shape=jax.ShapeDtypeStruct((m, d), F32),
        compiler_params=_cparams(("parallel",)),
        name="mlp",
    )(x2d, g.reshape(1, d), w1, w2)


def kernel(x, mem, norm_mix_g, ev_w_in, conv_dw_w, conv_dw_b, conv_ln_g, conv_ln_b,
           hgrn_lb_logits, hgrn_onorm_g, ev_w_out, od_w_in, sgu_ln_g, sgu_ln_b, sgu_w, sgu_b,
           attn_qnorm_g, attn_knorm_g, od_w_out, norm_xattn_g, norm_mem_g, xattn_wq, xattn_wkv,
           xattn_qnorm_g, xattn_knorm_g, xattn_wo, norm_mlp_g, mlp_w1, mlp_w2):
    batch, seq, d = x.shape
    bf = lambda a: a.astype(BF16)
    lb_all = jnp.cumsum(jax.nn.softmax(hgrn_lb_logits.astype(F32), axis=0), axis=0)
    lb_all = lb_all - lb_all[0]
    xf = x.reshape(batch * seq, d)
    for l in range(DEPTH):
        if l % 2 == 0:
            e = l // 2
            w_in = bf(ev_w_in[e])
            m1, qig, f_pre = _even_proj(xf, norm_mix_g[l], w_in, conv_dw_w[e], conv_dw_b[e],
                                        conv_ln_g[e], conv_ln_b[e], batch, seq)
            m2 = _hgrn2(qig, f_pre, lb_all[e], hgrn_onorm_g[e], batch, seq)
            w_out = bf(ev_w_out[e])
            wo1, wo2 = w_out[:CONV_DIM], w_out[CONV_DIM:]
        else:
            o = l // 2
            w_in = bf(od_w_in[o])
            (p,) = _norm_proj(xf, norm_mix_g[l], w_in, ODD_OUTS, ODD_PIECES)
            m1 = _chunked_sgu(p, sgu_ln_g[o], sgu_ln_b[o], sgu_w[o], sgu_b[o])
            m2 = _dilated_attention(p, attn_qnorm_g[o], attn_knorm_g[o], batch, seq)
            w_out = bf(od_w_out[o])
            wo1, wo2 = w_out[:SGU_DIM], w_out[SGU_DIM:]
        k_mem, v_mem = _mem_kv(mem, norm_mem_g[l], bf(xattn_wkv[l]), xattn_knorm_g[l], batch)
        xf = _xattn(xf, m1, m2, wo1, wo2, norm_xattn_g[l], bf(xattn_wq[l]), xattn_qnorm_g[l],
                    k_mem, v_mem, bf(xattn_wo[l]), seq)
        xf = _mlp(xf, norm_mlp_g[l], bf(mlp_w1[l]), bf(mlp_w2[l]))
    return xf.reshape(batch, seq, d)
```

```python
import functools

import numpy as np
import jax
import jax.numpy as jnp
from jax import lax
from jax.experimental import pallas as pl
from jax.experimental.pallas import tpu as pltpu

F32 = jnp.float32
BF16 = jnp.bfloat16

D_MODEL = 1024
DEPTH = 4
EPS = 1e-6
CONV_DIM = 512
CONV_WIDTH = 31
HGRN_DIM = 512
HGRN_HEAD_DIM = 128
HGRN_HEADS = 4
HGRN_CHUNK = 64
SGU_DIM = 512
SGU_GROUPS = 4
SGU_CHUNK = 128
ATTN_HEAD_DIM = 64
DIL_CONFIGS = ((128, 1), (512, 4), (2048, 16))
ATTN_HEADS_PER_CFG = 4
ATTN_CFG_DIM = ATTN_HEADS_PER_CFG * ATTN_HEAD_DIM
ATTN_DIM = ATTN_CFG_DIM * len(DIL_CONFIGS)
ATTN_BLOCK = 128
ATTN_SPAN = 128
ROPE_THETA = 500000.0
ROT_DIM = 16
MEM_LEN = 256
XATTN_HEADS = 4
XATTN_HEAD_DIM = 256
D_FF = 4096

LANES = 128
SUBLANES = 8
ROW_TILE = 1024
CONV_HALO = 32
ATTN_SUPER = 1024
ATTN_PREP_ROWS = 256
ATTN_GROUP = 8
NEG_BIG = -1e30
VMEM_LIMIT = 56 * 1024 * 1024


def _cparams(sem):
    return pltpu.CompilerParams(dimension_semantics=sem, vmem_limit_bytes=VMEM_LIMIT)


def _const_spec(shape):
    nd = len(shape)
    return pl.BlockSpec(shape, lambda *_: (0,) * nd, pipeline_mode=pl.Buffered(1))


def _dot(a, b):
    return jnp.dot(a, b, preferred_element_type=F32)


def _dot_nt(a, b):
    return lax.dot_general(a, b, (((1,), (1,)), ((), ())), preferred_element_type=F32)


def _dot_tn(a, b):
    return lax.dot_general(a, b, (((0,), (0,)), ((), ())), preferred_element_type=F32)


def _rms(x, g):
    return x * lax.rsqrt(jnp.mean(x * x, axis=-1, keepdims=True) + EPS) * g


def _layernorm(x, g, b):
    mu = jnp.mean(x, axis=-1, keepdims=True)
    xc = x - mu
    return xc * lax.rsqrt(jnp.mean(xc * xc, axis=-1, keepdims=True) + EPS) * g + b


def _sigmoid(x):
    return 1.0 / (1.0 + jnp.exp(-x))


def _norm_proj_body(x_ref, g_ref, w_ref, *o_refs, pieces):
    h = _rms(x_ref[...], g_ref[...]).astype(BF16)
    for out, w0, o0, n in pieces:
        o_refs[out][:, o0:o0 + n] = _dot(h, w_ref[:, w0:w0 + n]).astype(o_refs[out].dtype)


def _norm_proj(x2d, g, w, outs, pieces, tm=ROW_TILE):
    m, d = x2d.shape
    return pl.pallas_call(
        functools.partial(_norm_proj_body, pieces=pieces),
        grid=(m // tm,),
        in_specs=[pl.BlockSpec((tm, d), lambda i: (i, 0)),
                  _const_spec((1, d)), _const_spec(w.shape)],
        out_specs=[pl.BlockSpec((tm, n), lambda i: (i, 0)) for n, _ in outs],
        out_shape=[jax.ShapeDtypeStruct((m, n), dt) for n, dt in outs],
        compiler_params=_cparams(("parallel",)),
        name="norm_proj",
    )(x2d, g.reshape(1, d), w)


ODD_IN = 2 * SGU_DIM + 3 * ATTN_DIM
ODD_OUTS = ((ODD_IN, BF16),)
ODD_PIECES = ((0, 0, 0, 1024), (0, 1024, 1024, 768), (0, 1792, 1792, 768), (0, 2560, 2560, 768))


CONV_ROWS = 64
EVEN_ROW_TILE = 512


def _even_proj_body(x_ref, g_ref, w_ref, cw_ref, cb_ref, lg_ref, lb_ref,
                    a_ref, qig_ref, f_ref, hbuf, sh_ref, *, tm):
    c = CONV_DIM
    hd = HGRN_DIM
    n_rows = CONV_HALO + tm
    h = _rms(x_ref[...], g_ref[...]).astype(BF16)
    a_in = _dot(h, w_ref[:, 0:2 * c])
    c0 = 2 * c
    qig_ref[:, 0:hd] = _dot(h, w_ref[:, c0:c0 + hd]).astype(BF16)
    f_ref[...] = _dot(h, w_ref[:, c0 + hd:c0 + 2 * hd])
    qig_ref[:, hd:3 * hd] = _dot(h, w_ref[:, c0 + 2 * hd:c0 + 4 * hd]).astype(BF16)

    @pl.when(pl.program_id(1) == 0)
    def _():
        hbuf[0:CONV_HALO, :] = jnp.zeros((CONV_HALO, c), F32)

    hbuf[CONV_HALO:n_rows, :] = a_in[:, :c] * _sigmoid(a_in[:, c:])
    for s in range(1, SUBLANES):
        sh_ref[s, 0:n_rows - SUBLANES, :] = hbuf[s:s + n_rows - SUBLANES, :]
    base = CONV_HALO - (CONV_WIDTH - 1)
    for r0 in range(0, tm, CONV_ROWS):
        acc = jnp.zeros((CONV_ROWS, c), F32) + cb_ref[...]
        for j in range(CONV_WIDTH):
            s = (base + j) % SUBLANES
            a0 = base + j - s + r0
            tap = hbuf[a0:a0 + CONV_ROWS, :] if s == 0 else sh_ref[s, a0:a0 + CONV_ROWS, :]
            acc = acc + cw_ref[j:j + 1, :] * tap
        y = _layernorm(acc, lg_ref[...], lb_ref[...])
        a_ref[r0:r0 + CONV_ROWS, :] = (y * _sigmoid(y)).astype(a_ref.dtype)
    hbuf[0:CONV_HALO, :] = hbuf[tm:n_rows, :]


def _even_proj(x2d, g, w, dw_w, dw_b, ln_g, ln_b, batch, seq, tm=EVEN_ROW_TILE):
    m, d = x2d.shape
    c = CONV_DIM
    hd = HGRN_DIM
    w_pad = jnp.zeros((CONV_HALO, c), F32).at[:CONV_WIDTH].set(dw_w)
    per_b = seq // tm
    row = lambda n: pl.BlockSpec((tm, n), lambda b, t: (b * per_b + t, 0))
    return pl.pallas_call(
        functools.partial(_even_proj_body, tm=tm),
        grid=(batch, per_b),
        in_specs=[row(d), _const_spec((1, d)), _const_spec(w.shape),
                  _const_spec((CONV_HALO, c)), _const_spec((1, c)),
                  _const_spec((1, c)), _const_spec((1, c))],
        out_specs=[row(c), row(3 * hd), row(hd)],
        out_shape=[jax.ShapeDtypeStruct((m, c), BF16), jax.ShapeDtypeStruct((m, 3 * hd), BF16),
                   jax.ShapeDtypeStruct((m, hd), F32)],
        scratch_shapes=[pltpu.VMEM((CONV_HALO + tm, c), F32),
                        pltpu.VMEM((SUBLANES, CONV_HALO + tm, c), F32)],
        compiler_params=_cparams(("parallel", "arbitrary")),
        name="even_proj",
    )(x2d, g.reshape(1, d), w, w_pad, dw_b.reshape(1, c), ln_g.reshape(1, c), ln_b.reshape(1, c))


_HGRN_LEVELS = (32, 16, 8, 4, 2, 1)
_HGRN_MATRIX_LEVELS = (4, 2)
HGRN_GROUP = 4


def _hgrn_tables():
    n = HGRN_CHUNK
    t = np.arange(n)[:, None]
    u = np.arange(n)[None, :]
    blocks = [u <= t, u > t]
    masks = []
    offsets = {}
    row = 2 * n
    for m in _HGRN_LEVELS:
        c = (t // (2 * m)) * (2 * m)
        upper = (t - c) >= m
        if m in _HGRN_MATRIX_LEVELS:
            blocks.append(upper & (u >= c + m) & (u <= t))
            blocks.append((~upper) & (u >= t + 1) & (u <= c + m - 1))
            offsets[m] = (row, row + n)
            row += 2 * n
        cs = (u // (2 * m)) * (2 * m)
        masks.append(((t // (2 * m)) == (u // (2 * m))) & upper & ((u - cs) < m))
    d = np.concatenate(blocks, axis=0).astype(np.float32)
    return np.concatenate([d, d, d], axis=1), np.stack(masks).astype(np.float32), offsets


_HGRN_OFFSETS = _hgrn_tables()[2]


def _hgrn_body(q_ref, f_ref, i_ref, g_ref, lb_ref, og_ref, d_ref, m_ref, o_ref, st_ref,
               *, n_chunks):
    n = HGRN_CHUNK
    hd = HGRN_HEAD_DIM

    @pl.when(pl.program_id(1) == 0)
    def _():
        st_ref[...] = jnp.zeros(st_ref.shape, F32)

    lb = lb_ref[...]
    half = n // 2

    def level_operands(qh, kh, eh, li, m):
        if m == 1:
            return (qh * (1.0 - kh)).astype(BF16), kh.astype(BF16)
        if m in _HGRN_MATRIX_LEVELS:
            q0, k0 = _HGRN_OFFSETS[m]
            return ((qh * jnp.exp2(eh[q0:q0 + n])).astype(BF16),
                    (kh * jnp.exp2(eh[k0:k0 + n])).astype(BF16))
        e_b = eh[0:n]
        q_parts, k_parts = [], []
        for b in range(half // m):
            lo = slice(2 * b * m, (2 * b + 1) * m)
            up = slice((2 * b + 1) * m, (2 * b + 2) * m)
            ref = e_b[(2 * b + 1) * m - 1:(2 * b + 1) * m, :]
            q_parts += [qh[lo], qh[up] * jnp.exp2(e_b[up] - ref)]
            k_parts += [kh[lo] * jnp.exp2(ref - e_b[lo]), kh[up]]
        return (jnp.concatenate(q_parts, axis=0).astype(BF16),
                jnp.concatenate(k_parts, axis=0).astype(BF16))

    def group(gi, carry):
        chunks = []
        for ci in range(HGRN_GROUP):
            rows = pl.ds(pl.multiple_of((gi * HGRN_GROUP + ci) * n, n), n)
            f = lb + (1.0 - lb) * _sigmoid(f_ref[rows, :])
            logf = jnp.log2(f)
            hi = logf.astype(BF16)
            r1 = logf - hi.astype(F32)
            mid = r1.astype(BF16)
            lo = (r1 - mid.astype(F32)).astype(BF16)
            expo = _dot(d_ref[...], jnp.concatenate([hi, mid, lo], axis=0))
            chunks.append((rows, 1.0 - f, expo, q_ref[rows, :].astype(F32), i_ref[rows, :],
                           g_ref[rows, :].astype(F32)))
        items = []
        for rows, kk, expo, q, v, g in chunks:
            for h in range(HGRN_HEADS):
                ls = slice(h * hd, (h + 1) * hd)
                items.append((rows, ls, h, q[:, ls], kk[:, ls], v[:, ls], g[:, ls], expo[:, ls]))
        operands = [[level_operands(qh, kh, eh, li, m) for li, m in enumerate(_HGRN_LEVELS)]
                    for _, _, _, qh, kh, _, _, eh in items]
        q_inter = [(qh * jnp.exp2(eh[0:n])).astype(BF16) for _, _, _, qh, _, _, _, eh in items]
        k_state = [(kh * jnp.exp2(eh[n:2 * n])).astype(BF16) for _, _, _, _, kh, _, _, eh in items]
        scores = []
        for ops in operands:
            sc = jnp.zeros((n, n), F32)
            for li, (qs, ks) in enumerate(ops):
                sc = sc + m_ref[li] * _dot_nt(qs, ks)
            scores.append(sc.astype(BF16))
        intra = [_dot(sc, it[5]) for sc, it in zip(scores, items)]
        kv = [_dot_tn(it[5], ks) for ks, it in zip(k_state, items)]
        inter = []
        for idx, (_, _, h, _, _, _, _, eh) in enumerate(items):
            st = st_ref[h]
            inter.append(_dot_nt(q_inter[idx], st.astype(BF16)))
            st_ref[h] = st * jnp.exp2(eh[n - 1:n, :]) + kv[idx]
        for idx, (rows, ls, _, qh, kh, vh, gh, _) in enumerate(items):
            o = inter[idx] + intra[idx]
            o = o + jnp.sum(qh * kh, axis=-1, keepdims=True) * vh.astype(F32)
            on = _rms(o, og_ref[...])
            o_ref[rows, ls] = (on * (gh * _sigmoid(gh))).astype(o_ref.dtype)
        return carry

    lax.fori_loop(0, n_chunks // HGRN_GROUP, group, 0)


def _hgrn2(qig, f_pre, lb, onorm_g, batch, seq, tt=512):
    c = HGRN_DIM
    d3, masks, _ = _hgrn_tables()
    p3 = qig.reshape(batch, seq, qig.shape[1])
    f3 = f_pre.reshape(batch, seq, c)
    col = lambda j: pl.BlockSpec((None, tt, c), lambda b, t: (b, t, j))
    out = pl.pallas_call(
        functools.partial(_hgrn_body, n_chunks=tt // HGRN_CHUNK),
        grid=(batch, seq // tt),
        in_specs=[col(0), col(0), col(1), col(2),
                  _const_spec((1, c)), _const_spec((1, HGRN_HEAD_DIM)),
                  _const_spec(d3.shape), _const_spec(masks.shape)],
        out_specs=pl.BlockSpec((None, tt, c), lambda b, t: (b, t, 0)),
        out_shape=jax.ShapeDtypeStruct((batch, seq, c), BF16),
        scratch_shapes=[pltpu.VMEM((HGRN_HEADS, HGRN_HEAD_DIM, HGRN_HEAD_DIM), F32)],
        compiler_params=_cparams(("parallel", "arbitrary")),
        name="hgrn2",
    )(p3, f3, p3, p3, lb.reshape(1, c), onorm_g.reshape(1, HGRN_HEAD_DIM),
      jnp.asarray(d3, BF16), jnp.asarray(masks, F32))
    return out.reshape(batch * seq, c)


def _sgu_body(z_ref, lg_ref, lb_ref, w_ref, bias_ref, o_ref, *, tt):
    c = SGU_DIM
    gw = c // SGU_GROUPS
    z = z_ref[...].astype(F32)
    z = 0.5 * z * (1.0 + lax.erf(z * np.float32(1.0 / np.sqrt(2.0))))
    u = z[:, :c]
    v = _layernorm(z[:, c:], lg_ref[...], lb_ref[...]).astype(BF16)
    for ci in range(tt // SGU_CHUNK):
        rs = slice(ci * SGU_CHUNK, (ci + 1) * SGU_CHUNK)
        for gi in range(SGU_GROUPS):
            ls = slice(gi * gw, (gi + 1) * gw)
            mixed = _dot(w_ref[gi], v[rs, ls]) + bias_ref[:, ls]
            o_ref[rs, ls] = (u[rs, ls] * mixed).astype(o_ref.dtype)


def _chunked_sgu(p, ln_g, ln_b, w_s, b_s, tt=512):
    m = p.shape[0]
    c = SGU_DIM
    tril = np.tril(np.ones((SGU_CHUNK, SGU_CHUNK), dtype=bool))
    w = jnp.where(tril[None], w_s, 0.0).astype(BF16)
    bias = jnp.repeat(b_s.T, c // SGU_GROUPS, axis=1)
    return pl.pallas_call(
        functools.partial(_sgu_body, tt=tt),
        grid=(m // tt,),
        in_specs=[pl.BlockSpec((tt, 2 * c), lambda i: (i, 0)),
                  _const_spec((1, c)), _const_spec((1, c)),
                  _const_spec(w.shape), _const_spec((SGU_CHUNK, c))],
        out_specs=pl.BlockSpec((tt, c), lambda i: (i, 0)),
        out_shape=jax.ShapeDtypeStruct((m, c), BF16),
        compiler_params=_cparams(("parallel",)),
        name="chunked_sgu",
    )(p, ln_g.reshape(1, c), ln_b.reshape(1, c), w, bias)


def _rope_tables(seq):
    half = ROT_DIM // 2
    inv_freq = jnp.power(ROPE_THETA, -jnp.arange(half, dtype=F32) / half)
    ang = jnp.arange(seq, dtype=F32)[:, None] * inv_freq[None, :]
    cos, sin = jnp.cos(ang), jnp.sin(ang)
    ones = jnp.ones((seq, ATTN_HEAD_DIM - ROT_DIM), F32)
    zeros = jnp.zeros((seq, ATTN_HEAD_DIM - ROT_DIM), F32)
    zh = jnp.zeros((seq, half), F32)
    c_head = jnp.concatenate([cos, cos, ones], axis=1)
    s_dn_head = jnp.concatenate([zh, sin, zeros], axis=1)
    s_up_head = jnp.concatenate([-sin, zh, zeros], axis=1)
    tile = lambda a: jnp.concatenate([a, a], axis=1)
    return tile(c_head), tile(s_dn_head), tile(s_up_head)


def _attn_geometry(dilation):
    per_class = ATTN_SUPER // dilation
    rows = min(per_class, ATTN_BLOCK)
    return per_class, rows, per_class // rows


def _attn_prep(x_ref, g_ref, c_ref, sd_ref, su_ref, bd_ref, dst_ref):
    half = ROT_DIM // 2
    for pair in range(ATTN_CFG_DIM // LANES):
        ls = slice(pair * LANES, (pair + 1) * LANES)
        for r0 in range(0, ATTN_SUPER, ATTN_PREP_ROWS):
            rr = slice(r0, r0 + ATTN_PREP_ROWS)
            x = x_ref[rr, ls].astype(F32)
            ms = _dot((x * x).astype(BF16), bd_ref[...])
            y = x * lax.rsqrt(ms + EPS) * g_ref[:, ls]
            dst_ref[pair, rr, :] = (y * c_ref[rr, :] + pltpu.roll(y, half, 1) * sd_ref[rr, :]
                                    + pltpu.roll(y, LANES - half, 1) * su_ref[rr, :])


def _attn_band(rows):
    blk = ATTN_BLOCK
    qi = lax.broadcasted_iota(jnp.int32, (2 * rows, blk + rows), 0) % rows
    key_col = lax.broadcasted_iota(jnp.int32, (2 * rows, blk + rows), 1)
    dist = qi - (key_col - blk)
    band_bias = jnp.where((dist >= 0) & (dist <= ATTN_SPAN), 0.0, NEG_BIG).astype(F32)
    return band_bias, key_col


def _attn_units(dilation, units, step, band_bias, key_col, qd_ref, kd_ref, vd_ref, od_ref, ld_ref):
    per_class, rows, _ = _attn_geometry(dilation)
    blk = ATTN_BLOCK
    head0 = lax.broadcasted_iota(jnp.int32, (1, LANES), 1) < ATTN_HEAD_DIM
    items = []
    for r, j in units:
        q_rows = pl.ds(pl.multiple_of(r * per_class + j * blk, rows), rows)
        k_rows = pl.ds(pl.multiple_of(j * blk, blk), blk + rows)
        first_key = jnp.maximum(blk - (step * per_class + j * blk), 0)
        bias = jnp.where(key_col >= first_key, band_bias, NEG_BIG)
        for pair in range(ATTN_CFG_DIM // LANES):
            items.append((pair, r, q_rows, k_rows, bias))
    scores = []
    for pair, r, q_rows, k_rows, bias in items:
        qp = qd_ref[pair, q_rows, :]
        zero = jnp.zeros_like(qp)
        q2 = jnp.concatenate([jnp.where(head0, qp, zero), jnp.where(head0, zero, qp)], axis=0)
        scores.append(_dot_nt(q2, kd_ref[pair, r, k_rows, :]) + bias)
    probs = []
    for s in scores:
        mx = jnp.max(s, axis=-1, keepdims=True)
        p = jnp.exp2(s - mx)
        den = jnp.sum(p, axis=-1, keepdims=True)
        probs.append((p.astype(BF16), 1.0 / den, mx + jnp.log2(den)))
    outs = [_dot(p, vd_ref[pair, r, k_rows, :]) * inv
            for (p, inv, _), (pair, r, _, k_rows, _) in zip(probs, items)]
    for o2, (_, _, lse), (pair, _, q_rows, _, _) in zip(outs, probs, items):
        od_ref[pair, q_rows, :] = jnp.where(head0, o2[:rows], o2[rows:])
        ld_ref[pair, q_rows, :] = jnp.where(head0, lse[:rows], lse[rows:])


def _attn_config(cfg, dilation, step, ks_ref, v_ref, vs_ref, qs_ref, qd_ref, od_ref, ld_ref,
                 kd_ref, vd_ref, oacc_ref, lacc_ref):
    per_class, rows, n_blk = _attn_geometry(dilation)
    blk = ATTN_BLOCK
    pairs = ATTN_CFG_DIM // LANES

    @pl.when(step == 0)
    def _():
        zeros = jnp.zeros((pairs, dilation, blk, LANES), BF16)
        kd_ref[:, :, 0:blk, :] = zeros
        vd_ref[:, :, 0:blk, :] = zeros

    cur = slice(blk, blk + per_class)
    for pair in range(pairs):
        ls = slice(pair * LANES, (pair + 1) * LANES)
        if dilation == 1:
            qd_ref[pair] = qs_ref[pair].astype(BF16)
            kd_ref[pair, 0, cur, :] = ks_ref[pair].astype(BF16)
            vd_ref[pair, 0, cur, :] = v_ref[:, ls]
        else:
            vs_ref[pair] = v_ref[:, ls].astype(F32)
            for r in range(dilation):
                src = pl.ds(r, per_class, stride=dilation)
                qd_ref[pair, r * per_class:(r + 1) * per_class, :] = (
                    qs_ref[pair, src, :].astype(BF16))
                kd_ref[pair, r, cur, :] = ks_ref[pair, src, :].astype(BF16)
                vd_ref[pair, r, cur, :] = vs_ref[pair, src, :].astype(BF16)

    band_bias, key_col = _attn_band(rows)

    def group(gi, carry):
        units = [((gi * ATTN_GROUP + i) % dilation, (gi * ATTN_GROUP + i) // dilation)
                 for i in range(ATTN_GROUP)]
        _attn_units(dilation, units, step, band_bias, key_col,
                    qd_ref, kd_ref, vd_ref, od_ref, ld_ref)
        return carry

    lax.fori_loop(0, dilation * n_blk // ATTN_GROUP, group, 0)

    for pair in range(pairs):
        for r in range(dilation):
            src = slice(r * per_class, (r + 1) * per_class)
            dst = pl.ds(r, per_class, stride=dilation) if dilation > 1 else slice(0, per_class)
            oacc_ref[cfg, pair, dst, :] = od_ref[pair, src, :]
            lacc_ref[cfg, pair, dst, :] = ld_ref[pair, src, :]

    for ref in (kd_ref, vd_ref):
        ref[:, :, 0:blk, :] = ref[:, :, per_class:per_class + blk, :]


def _dil_attn_body(q_ref, k_ref, v_ref, qg_ref, kg_ref, c_ref, sd_ref, su_ref, bd_ref,
                   o_ref, qs_ref, ks_ref, vs_ref, od_ref, ld_ref, qd_ref,
                   kd0, kd1, kd2, vd0, vd1, vd2, oacc_ref, lacc_ref):
    step = pl.program_id(1)
    cfg_id = pl.program_id(2)
    n_cfg = len(DIL_CONFIGS)
    _attn_prep(q_ref, qg_ref, c_ref, sd_ref, su_ref, bd_ref, qs_ref)
    _attn_prep(k_ref, kg_ref, c_ref, sd_ref, su_ref, bd_ref, ks_ref)
    kds, vds = (kd0, kd1, kd2), (vd0, vd1, vd2)
    for cfg, (_, dilation) in enumerate(DIL_CONFIGS):
        @pl.when(cfg_id == cfg)
        def _(cfg=cfg, dilation=dilation):
            _attn_config(cfg, dilation, step, ks_ref, v_ref, vs_ref, qs_ref, qd_ref, od_ref,
                         ld_ref, kds[cfg], vds[cfg], oacc_ref, lacc_ref)

    @pl.when(cfg_id == n_cfg - 1)
    def _():
        for pair in range(ATTN_CFG_DIM // LANES):
            for r0 in range(0, ATTN_SUPER, ATTN_PREP_ROWS):
                rr = slice(r0, r0 + ATTN_PREP_ROWS)
                ls = [lacc_ref[c, pair, rr, :] for c in range(n_cfg)]
                mx = jnp.maximum(jnp.maximum(ls[0], ls[1]), ls[2])
                es = [jnp.exp2(l - mx) for l in ls]
                inv = 1.0 / (es[0] + es[1] + es[2])
                for c in range(n_cfg):
                    col = c * ATTN_CFG_DIM + pair * LANES
                    o_ref[rr, col:col + LANES] = (
                        oacc_ref[c, pair, rr, :] * (es[c] * inv)).astype(o_ref.dtype)


def _dilated_attention(p, qn_g, kn_g, batch, seq):
    m = p.shape[0]
    n_cfg = len(DIL_CONFIGS)
    n_sb = seq // ATTN_SUPER
    pairs = ATTN_CFG_DIM // LANES
    c_t, sd_t, su_t = _rope_tables(seq)
    head = np.arange(LANES) // ATTN_HEAD_DIM
    bd = jnp.asarray((head[:, None] == head[None, :]).astype(np.float32) / ATTN_HEAD_DIM, BF16)
    tile_g = lambda g: jnp.tile(g, ATTN_CFG_DIM // ATTN_HEAD_DIM).reshape(1, ATTN_CFG_DIM)
    qg = tile_g(qn_g) * np.float32(ATTN_HEAD_DIM ** -0.5 * np.log2(np.e))
    q0 = 2 * SGU_DIM // ATTN_CFG_DIM
    k0 = q0 + n_cfg
    v0 = k0 + n_cfg
    wide = lambda c0: pl.BlockSpec((ATTN_SUPER, ATTN_CFG_DIM),
                                   lambda b, s, c: (b * n_sb + s, c0 + c))
    tab = pl.BlockSpec((ATTN_SUPER, LANES), lambda b, s, c: (s, 0))
    hist = lambda d: pltpu.VMEM((pairs, d, ATTN_BLOCK + ATTN_SUPER // d, LANES), BF16)
    dils = [d for _, d in DIL_CONFIGS]
    return pl.pallas_call(
        _dil_attn_body,
        grid=(batch, n_sb, n_cfg),
        in_specs=[wide(q0), wide(k0), wide(v0),
                  _const_spec((1, ATTN_CFG_DIM)), _const_spec((1, ATTN_CFG_DIM)),
                  tab, tab, tab, _const_spec((LANES, LANES))],
        out_specs=pl.BlockSpec((ATTN_SUPER, ATTN_DIM), lambda b, s, c: (b * n_sb + s, 0)),
        out_shape=jax.ShapeDtypeStruct((m, ATTN_DIM), BF16),
        scratch_shapes=[pltpu.VMEM((pairs, ATTN_SUPER, LANES), F32)] * 5
        + [pltpu.VMEM((pairs, ATTN_SUPER, LANES), BF16)]
        + [hist(d) for d in dils] + [hist(d) for d in dils]
        + [pltpu.VMEM((n_cfg, pairs, ATTN_SUPER, LANES), F32)] * 2,
        compiler_params=_cparams(("parallel", "arbitrary", "arbitrary")),
        name="dil_attn",
    )(p, p, p, qg, tile_g(kn_g), c_t, sd_t, su_t, bd)


def _mem_kv_body(mem_ref, g_ref, w_ref, kg_ref, k_ref, v_ref):
    d = D_MODEL
    h = _rms(mem_ref[...], g_ref[...]).astype(BF16)
    for hh in range(XATTN_HEADS):
        ls = slice(hh * XATTN_HEAD_DIM, (hh + 1) * XATTN_HEAD_DIM)
        kh = _dot(h, w_ref[:, ls])
        k_ref[:, ls] = _rms(kh, kg_ref[...]).astype(k_ref.dtype)
    v_ref[...] = _dot(h, w_ref[:, d:]).astype(v_ref.dtype)


def _mem_kv(mem, g, wkv, kn_g, batch):
    d = D_MODEL
    spec = pl.BlockSpec((None, MEM_LEN, d), lambda b: (b, 0, 0))
    return pl.pallas_call(
        _mem_kv_body,
        grid=(batch,),
        in_specs=[spec, _const_spec((1, d)), _const_spec((d, 2 * d)),
                  _const_spec((1, XATTN_HEAD_DIM))],
        out_specs=[spec, spec],
        out_shape=[jax.ShapeDtypeStruct((batch, MEM_LEN, d), BF16)] * 2,
        compiler_params=_cparams(("parallel",)),
        name="mem_kv",
    )(mem, g.reshape(1, d), wkv, kn_g.reshape(1, XATTN_HEAD_DIM))


def _xattn_body(x_ref, m1_ref, m2_ref, wm_ref, gx_ref, wq_ref, qg_ref, k_ref, v_ref,
                wo_ref, o_ref):
    n1 = m1_ref.shape[1]
    x1 = (x_ref[...] + _dot(m1_ref[...], wm_ref[0:n1, :])
          + _dot(m2_ref[...], wm_ref[n1:, :]))
    q = _dot(_rms(x1, gx_ref[...]).astype(BF16), wq_ref[...])
    heads = []
    for hh in range(XATTN_HEADS):
        ls = slice(hh * XATTN_HEAD_DIM, (hh + 1) * XATTN_HEAD_DIM)
        qh = _rms(q[:, ls], qg_ref[...]).astype(BF16)
        s = _dot_nt(qh, k_ref[:, ls])
        p = jnp.exp(s - jnp.max(s, axis=-1, keepdims=True))
        inv = 1.0 / jnp.sum(p, axis=-1, keepdims=True)
        heads.append((_dot(p.astype(BF16), v_ref[:, ls]) * inv).astype(BF16))
    o_ref[...] = x1 + _dot(jnp.concatenate(heads, axis=1), wo_ref[...])


def _xattn(x2d, m1, m2, w_mix, gx, wq, qn_g, k_mem, v_mem, wo, seq, tm=ROW_TILE):
    m, d = x2d.shape
    per_b = seq // tm
    row = lambda n: pl.BlockSpec((tm, n), lambda i: (i, 0))
    mem = pl.BlockSpec((None, MEM_LEN, d), lambda i: (i // per_b, 0, 0))
    qg = (qn_g * np.float32(XATTN_HEAD_DIM ** -0.5)).reshape(1, XATTN_HEAD_DIM)
    return pl.pallas_call(
        _xattn_body,
        grid=(m // tm,),
        in_specs=[row(d), row(m1.shape[1]), row(m2.shape[1]),
                  _const_spec(w_mix.shape), _const_spec((1, d)),
                  _const_spec(wq.shape), _const_spec((1, XATTN_HEAD_DIM)), mem, mem,
                  _const_spec(wo.shape)],
        out_specs=row(d),
        out_shape=jax.ShapeDtypeStruct((m, d), F32),
        compiler_params=_cparams(("parallel",)),
        name="xattn",
    )(x2d, m1, m2, w_mix, gx.reshape(1, d), wq, qg, k_mem, v_mem, wo)


def _mlp_body(x_ref, g_ref, w1_ref, w2_ref, o_ref, *, ff_tile):
    x = x_ref[...]
    h = _rms(x, g_ref[...]).astype(BF16)
    acc = x
    for c0 in range(0, D_FF, ff_tile):
        hid = jnp.maximum(_dot(h, w1_ref[:, c0:c0 + ff_tile]), 0.0)
        acc = acc + _dot((hid * hid).astype(BF16), w2_ref[c0:c0 + ff_tile, :])
    o_ref[...] = acc


def _mlp(x2d, g, w1, w2, tm=ROW_TILE, ff_tile=1024):
    m, d = x2d.shape
    row = pl.BlockSpec((tm, d), lambda i: (i, 0))
    return pl.pallas_call(
        functools.partial(_mlp_body, ff_tile=ff_tile),
        grid=(m // tm,),
        in_specs=[row, _const_spec((1, d)), _const_spec(w1.shape), _const_spec(w2.shape)],
        out_specs=row,
        out_shape=jax.ShapeDtypeStruct((m, d), F32),
        compiler_params=_cparams(("parallel",)),
        name="mlp",
    )(x2d, g.reshape(1, d), w1, w2)


def kernel(x, mem, norm_mix_g, ev_w_in, conv_dw_w, conv_dw_b, conv_ln_g, conv_ln_b,
           hgrn_lb_logits, hgrn_onorm_g, ev_w_out, od_w_in, sgu_ln_g, sgu_ln_b, sgu_w, sgu_b,
           attn_qnorm_g, attn_knorm_g, od_w_out, norm_xattn_g, norm_mem_g, xattn_wq, xattn_wkv,
           xattn_qnorm_g, xattn_knorm_g, xattn_wo, norm_mlp_g, mlp_w1, mlp_w2):
    batch, seq, d = x.shape
    (ev_w_in, ev_w_out, od_w_in, od_w_out, xattn_wq, xattn_wkv, xattn_wo, mlp_w1, mlp_w2) = (
        a.astype(BF16) for a in (ev_w_in, ev_w_out, od_w_in, od_w_out, xattn_wq, xattn_wkv,
                                 xattn_wo, mlp_w1, mlp_w2))
    lb_all = jnp.cumsum(jax.nn.softmax(hgrn_lb_logits.astype(F32), axis=0), axis=0)
    lb_all = lb_all - lb_all[0]
    xf = x.reshape(batch * seq, d)
    for l in range(DEPTH):
        if l % 2 == 0:
            e = l // 2
            m1, qig, f_pre = _even_proj(xf, norm_mix_g[l], ev_w_in[e], conv_dw_w[e], conv_dw_b[e],
                                        conv_ln_g[e], conv_ln_b[e], batch, seq)
            m2 = _hgrn2(qig, f_pre, lb_all[e], hgrn_onorm_g[e], batch, seq)
            w_out = ev_w_out[e]
        else:
            o = l // 2
            (p,) = _norm_proj(xf, norm_mix_g[l], od_w_in[o], ODD_OUTS, ODD_PIECES)
            m1 = _chunked_sgu(p, sgu_ln_g[o], sgu_ln_b[o], sgu_w[o], sgu_b[o])
            m2 = _dilated_attention(p, attn_qnorm_g[o], attn_knorm_g[o], batch, seq)
            w_out = od_w_out[o]
        k_mem, v_mem = _mem_kv(mem, norm_mem_g[l], xattn_wkv[l], xattn_knorm_g[l], batch)
        xf = _xattn(xf, m1, m2, w_out, norm_xattn_g[l], xattn_wq[l], xattn_qnorm_g[l],
                    k_mem, v_mem, xattn_wo[l], seq)
        xf = _mlp(xf, norm_mlp_g[l], mlp_w1[l], mlp_w2[l])
    return xf.reshape(batch, seq, d)
```

```python
import functools

import numpy as np
import jax
import jax.numpy as jnp
from jax import lax
from jax.experimental import pallas as pl
from jax.experimental.pallas import tpu as pltpu

F32 = jnp.float32
BF16 = jnp.bfloat16

D_MODEL = 1024
DEPTH = 4
EPS = 1e-6
CONV_DIM = 512
CONV_WIDTH = 31
HGRN_DIM = 512
HGRN_HEAD_DIM = 128
HGRN_HEADS = 4
HGRN_CHUNK = 64
SGU_DIM = 512
SGU_GROUPS = 4
SGU_CHUNK = 128
ATTN_HEAD_DIM = 64
DIL_CONFIGS = ((128, 1), (512, 4), (2048, 16))
ATTN_HEADS_PER_CFG = 4
ATTN_CFG_DIM = ATTN_HEADS_PER_CFG * ATTN_HEAD_DIM
ATTN_DIM = ATTN_CFG_DIM * len(DIL_CONFIGS)
ATTN_BLOCK = 128
ATTN_SPAN = 128
ROPE_THETA = 500000.0
ROT_DIM = 16
MEM_LEN = 256
XATTN_HEADS = 4
XATTN_HEAD_DIM = 256
D_FF = 4096

LANES = 128
SUBLANES = 8
ROW_TILE = 1024
CONV_HALO = 32
ATTN_SUPER = 1024
ATTN_PREP_ROWS = 256
ATTN_GROUP = 8
NEG_BIG = -1e30
VMEM_LIMIT = 56 * 1024 * 1024


def _cparams(sem):
    return pltpu.CompilerParams(dimension_semantics=sem, vmem_limit_bytes=VMEM_LIMIT)


def _const_spec(shape):
    nd = len(shape)
    return pl.BlockSpec(shape, lambda *_: (0,) * nd, pipeline_mode=pl.Buffered(1))


def _layer_weight(stack, layer):
    shape = stack.shape[1:]
    index = (layer,) + (0,) * len(shape)
    spec = pl.BlockSpec((None,) + shape, lambda *_: index, pipeline_mode=pl.Buffered(1))
    return spec, stack


def _dot(a, b):
    return jnp.dot(a, b, preferred_element_type=F32)


def _dot_nt(a, b):
    return lax.dot_general(a, b, (((1,), (1,)), ((), ())), preferred_element_type=F32)


def _dot_tn(a, b):
    return lax.dot_general(a, b, (((0,), (0,)), ((), ())), preferred_element_type=F32)


def _rms(x, g):
    return x * lax.rsqrt(jnp.mean(x * x, axis=-1, keepdims=True) + EPS) * g


def _layernorm(x, g, b):
    mu = jnp.mean(x, axis=-1, keepdims=True)
    xc = x - mu
    return xc * lax.rsqrt(jnp.mean(xc * xc, axis=-1, keepdims=True) + EPS) * g + b


def _sigmoid(x):
    return 1.0 / (1.0 + jnp.exp(-x))


def _norm_proj_body(x_ref, g_ref, w_ref, *o_refs, pieces):
    h = _rms(x_ref[...], g_ref[...]).astype(BF16)
    for out, w0, o0, n in pieces:
        o_refs[out][:, o0:o0 + n] = _dot(h, w_ref[:, w0:w0 + n]).astype(o_refs[out].dtype)


def _norm_proj(x2d, g, w, outs, pieces, tm=ROW_TILE):
    m, d = x2d.shape
    return pl.pallas_call(
        functools.partial(_norm_proj_body, pieces=pieces),
        grid=(m // tm,),
        in_specs=[pl.BlockSpec((tm, d), lambda i: (i, 0)),
                  _const_spec((1, d)), w[0]],
        out_specs=[pl.BlockSpec((tm, n), lambda i: (i, 0)) for n, _ in outs],
        out_shape=[jax.ShapeDtypeStruct((m, n), dt) for n, dt in outs],
        compiler_params=_cparams(("parallel",)),
        name="norm_proj",
    )(x2d, g.reshape(1, d), w[1])


ODD_IN = 2 * SGU_DIM + 3 * ATTN_DIM
ODD_OUTS = ((ODD_IN, BF16),)
ODD_PIECES = ((0, 0, 0, 1024), (0, 1024, 1024, 768), (0, 1792, 1792, 768), (0, 2560, 2560, 768))


CONV_ROWS = 64
EVEN_ROW_TILE = 512


def _even_proj_body(x_ref, g_ref, w_ref, cw_ref, cb_ref, lg_ref, lb_ref,
                    a_ref, qig_ref, f_ref, hbuf, sh_ref, *, tm):
    c = CONV_DIM
    hd = HGRN_DIM
    n_rows = CONV_HALO + tm
    h = _rms(x_ref[...], g_ref[...]).astype(BF16)
    a_in = _dot(h, w_ref[:, 0:2 * c])
    c0 = 2 * c
    qig_ref[:, 0:hd] = _dot(h, w_ref[:, c0:c0 + hd]).astype(BF16)
    f_ref[...] = _dot(h, w_ref[:, c0 + hd:c0 + 2 * hd])
    qig_ref[:, hd:3 * hd] = _dot(h, w_ref[:, c0 + 2 * hd:c0 + 4 * hd]).astype(BF16)

    @pl.when(pl.program_id(1) == 0)
    def _():
        hbuf[0:CONV_HALO, :] = jnp.zeros((CONV_HALO, c), F32)

    hbuf[CONV_HALO:n_rows, :] = a_in[:, :c] * _sigmoid(a_in[:, c:])
    for s in range(1, SUBLANES):
        sh_ref[s, 0:n_rows - SUBLANES, :] = hbuf[s:s + n_rows - SUBLANES, :]
    base = CONV_HALO - (CONV_WIDTH - 1)
    for r0 in range(0, tm, CONV_ROWS):
        acc = jnp.zeros((CONV_ROWS, c), F32) + cb_ref[...]
        for j in range(CONV_WIDTH):
            s = (base + j) % SUBLANES
            a0 = base + j - s + r0
            tap = hbuf[a0:a0 + CONV_ROWS, :] if s == 0 else sh_ref[s, a0:a0 + CONV_ROWS, :]
            acc = acc + cw_ref[j:j + 1, :] * tap
        y = _layernorm(acc, lg_ref[...], lb_ref[...])
        a_ref[r0:r0 + CONV_ROWS, :] = (y * _sigmoid(y)).astype(a_ref.dtype)
    hbuf[0:CONV_HALO, :] = hbuf[tm:n_rows, :]


def _even_proj(x2d, g, w, dw_w, dw_b, ln_g, ln_b, batch, seq, tm=EVEN_ROW_TILE):
    m, d = x2d.shape
    c = CONV_DIM
    hd = HGRN_DIM
    w_pad = jnp.zeros((CONV_HALO, c), F32).at[:CONV_WIDTH].set(dw_w)
    per_b = seq // tm
    row = lambda n: pl.BlockSpec((tm, n), lambda b, t: (b * per_b + t, 0))
    return pl.pallas_call(
        functools.partial(_even_proj_body, tm=tm),
        grid=(batch, per_b),
        in_specs=[row(d), _const_spec((1, d)), w[0],
                  _const_spec((CONV_HALO, c)), _const_spec((1, c)),
                  _const_spec((1, c)), _const_spec((1, c))],
        out_specs=[row(c), row(3 * hd), row(hd)],
        out_shape=[jax.ShapeDtypeStruct((m, c), BF16), jax.ShapeDtypeStruct((m, 3 * hd), BF16),
                   jax.ShapeDtypeStruct((m, hd), F32)],
        scratch_shapes=[pltpu.VMEM((CONV_HALO + tm, c), F32),
                        pltpu.VMEM((SUBLANES, CONV_HALO + tm, c), F32)],
        compiler_params=_cparams(("parallel", "arbitrary")),
        name="even_proj",
    )(x2d, g.reshape(1, d), w[1], w_pad, dw_b.reshape(1, c), ln_g.reshape(1, c),
      ln_b.reshape(1, c))


_HGRN_LEVELS = (32, 16, 8, 4, 2, 1)
_HGRN_MATRIX_LEVELS = (4, 2)
HGRN_GROUP = 4


def _hgrn_tables():
    n = HGRN_CHUNK
    t = np.arange(n)[:, None]
    u = np.arange(n)[None, :]
    blocks = [u <= t, u > t]
    masks = []
    offsets = {}
    row = 2 * n
    for m in _HGRN_LEVELS:
        c = (t // (2 * m)) * (2 * m)
        upper = (t - c) >= m
        if m in _HGRN_MATRIX_LEVELS:
            blocks.append(upper & (u >= c + m) & (u <= t))
            blocks.append((~upper) & (u >= t + 1) & (u <= c + m - 1))
            offsets[m] = (row, row + n)
            row += 2 * n
        cs = (u // (2 * m)) * (2 * m)
        masks.append(((t // (2 * m)) == (u // (2 * m))) & upper & ((u - cs) < m))
    d = np.concatenate(blocks, axis=0).astype(np.float32)
    return np.concatenate([d, d, d], axis=1), np.stack(masks).astype(np.float32), offsets


_HGRN_OFFSETS = _hgrn_tables()[2]


def _hgrn_body(q_ref, f_ref, i_ref, g_ref, lb_ref, og_ref, d_ref, m_ref, o_ref, st_ref,
               *, n_chunks):
    n = HGRN_CHUNK
    hd = HGRN_HEAD_DIM

    @pl.when(pl.program_id(1) == 0)
    def _():
        st_ref[...] = jnp.zeros(st_ref.shape, F32)

    lb = lb_ref[...]
    half = n // 2

    def level_operands(qh, kh, eh, li, m):
        if m == 1:
            return (qh * (1.0 - kh)).astype(BF16), kh.astype(BF16)
        if m in _HGRN_MATRIX_LEVELS:
            q0, k0 = _HGRN_OFFSETS[m]
            return ((qh * jnp.exp2(eh[q0:q0 + n])).astype(BF16),
                    (kh * jnp.exp2(eh[k0:k0 + n])).astype(BF16))
        e_b = eh[0:n]
        q_parts, k_parts = [], []
        for b in range(half // m):
            lo = slice(2 * b * m, (2 * b + 1) * m)
            up = slice((2 * b + 1) * m, (2 * b + 2) * m)
            ref = e_b[(2 * b + 1) * m - 1:(2 * b + 1) * m, :]
            q_parts += [qh[lo], qh[up] * jnp.exp2(e_b[up] - ref)]
            k_parts += [kh[lo] * jnp.exp2(ref - e_b[lo]), kh[up]]
        return (jnp.concatenate(q_parts, axis=0).astype(BF16),
                jnp.concatenate(k_parts, axis=0).astype(BF16))

    def group(gi, carry):
        chunks = []
        for ci in range(HGRN_GROUP):
            rows = pl.ds(pl.multiple_of((gi * HGRN_GROUP + ci) * n, n), n)
            f = lb + (1.0 - lb) * _sigmoid(f_ref[rows, :])
            logf = jnp.log2(f)
            hi = logf.astype(BF16)
            r1 = logf - hi.astype(F32)
            mid = r1.astype(BF16)
            lo = (r1 - mid.astype(F32)).astype(BF16)
            expo = _dot(d_ref[...], jnp.concatenate([hi, mid, lo], axis=0))
            chunks.append((rows, 1.0 - f, expo, q_ref[rows, :].astype(F32), i_ref[rows, :],
                           g_ref[rows, :].astype(F32)))
        items = []
        for rows, kk, expo, q, v, g in chunks:
            for h in range(HGRN_HEADS):
                ls = slice(h * hd, (h + 1) * hd)
                items.append((rows, ls, h, q[:, ls], kk[:, ls], v[:, ls], g[:, ls], expo[:, ls]))
        operands = [[level_operands(qh, kh, eh, li, m) for li, m in enumerate(_HGRN_LEVELS)]
                    for _, _, _, qh, kh, _, _, eh in items]
        q_inter = [(qh * jnp.exp2(eh[0:n])).astype(BF16) for _, _, _, qh, _, _, _, eh in items]
        k_state = [(kh * jnp.exp2(eh[n:2 * n])).astype(BF16) for _, _, _, _, kh, _, _, eh in items]
        scores = []
        for ops in operands:
            sc = jnp.zeros((n, n), F32)
            for li, (qs, ks) in enumerate(ops):
                sc = sc + m_ref[li] * _dot_nt(qs, ks)
            scores.append(sc.astype(BF16))
        intra = [_dot(sc, it[5]) for sc, it in zip(scores, items)]
        kv = [_dot_tn(it[5], ks) for ks, it in zip(k_state, items)]
        inter = []
        for idx, (_, _, h, _, _, _, _, eh) in enumerate(items):
            st = st_ref[h]
            inter.append(_dot_nt(q_inter[idx], st.astype(BF16)))
            st_ref[h] = st * jnp.exp2(eh[n - 1:n, :]) + kv[idx]
        for idx, (rows, ls, _, qh, kh, vh, gh, _) in enumerate(items):
            o = inter[idx] + intra[idx]
            o = o + jnp.sum(qh * kh, axis=-1, keepdims=True) * vh.astype(F32)
            on = _rms(o, og_ref[...])
            o_ref[rows, ls] = (on * (gh * _sigmoid(gh))).astype(o_ref.dtype)
        return carry

    lax.fori_loop(0, n_chunks // HGRN_GROUP, group, 0)


def _hgrn2(qig, f_pre, lb, onorm_g, batch, seq, tt=1024):
    c = HGRN_DIM
    d3, masks, _ = _hgrn_tables()
    p3 = qig.reshape(batch, seq, qig.shape[1])
    f3 = f_pre.reshape(batch, seq, c)
    col = lambda j: pl.BlockSpec((None, tt, c), lambda b, t: (b, t, j))
    out = pl.pallas_call(
        functools.partial(_hgrn_body, n_chunks=tt // HGRN_CHUNK),
        grid=(batch, seq // tt),
        in_specs=[col(0), col(0), col(1), col(2),
                  _const_spec((1, c)), _const_spec((1, HGRN_HEAD_DIM)),
                  _const_spec(d3.shape), _const_spec(masks.shape)],
        out_specs=pl.BlockSpec((None, tt, c), lambda b, t: (b, t, 0)),
        out_shape=jax.ShapeDtypeStruct((batch, seq, c), BF16),
        scratch_shapes=[pltpu.VMEM((HGRN_HEADS, HGRN_HEAD_DIM, HGRN_HEAD_DIM), F32)],
        compiler_params=_cparams(("parallel", "arbitrary")),
        name="hgrn2",
    )(p3, f3, p3, p3, lb.reshape(1, c), onorm_g.reshape(1, HGRN_HEAD_DIM),
      jnp.asarray(d3, BF16), jnp.asarray(masks, F32))
    return out.reshape(batch * seq, c)


def _sgu_body(z_ref, lg_ref, lb_ref, w_ref, bias_ref, o_ref, *, tt):
    c = SGU_DIM
    gw = c // SGU_GROUPS
    z = z_ref[...].astype(F32)
    z = 0.5 * z * (1.0 + lax.erf(z * np.float32(1.0 / np.sqrt(2.0))))
    u = z[:, :c]
    v = _layernorm(z[:, c:], lg_ref[...], lb_ref[...]).astype(BF16)
    for ci in range(tt // SGU_CHUNK):
        rs = slice(ci * SGU_CHUNK, (ci + 1) * SGU_CHUNK)
        for gi in range(SGU_GROUPS):
            ls = slice(gi * gw, (gi + 1) * gw)
            mixed = _dot(w_ref[gi], v[rs, ls]) + bias_ref[:, ls]
            o_ref[rs, ls] = (u[rs, ls] * mixed).astype(o_ref.dtype)


def _chunked_sgu(p, ln_g, ln_b, w_s, b_s, tt=512):
    m = p.shape[0]
    c = SGU_DIM
    tril = np.tril(np.ones((SGU_CHUNK, SGU_CHUNK), dtype=bool))
    w = jnp.where(tril[None], w_s, 0.0).astype(BF16)
    bias = jnp.repeat(b_s.T, c // SGU_GROUPS, axis=1)
    return pl.pallas_call(
        functools.partial(_sgu_body, tt=tt),
        grid=(m // tt,),
        in_specs=[pl.BlockSpec((tt, 2 * c), lambda i: (i, 0)),
                  _const_spec((1, c)), _const_spec((1, c)),
                  _const_spec(w.shape), _const_spec((SGU_CHUNK, c))],
        out_specs=pl.BlockSpec((tt, c), lambda i: (i, 0)),
        out_shape=jax.ShapeDtypeStruct((m, c), BF16),
        compiler_params=_cparams(("parallel",)),
        name="chunked_sgu",
    )(p, ln_g.reshape(1, c), ln_b.reshape(1, c), w, bias)


def _rope_tables(seq):
    half = ROT_DIM // 2
    inv_freq = jnp.power(ROPE_THETA, -jnp.arange(half, dtype=F32) / half)
    ang = jnp.arange(seq, dtype=F32)[:, None] * inv_freq[None, :]
    cos, sin = jnp.cos(ang), jnp.sin(ang)
    ones = jnp.ones((seq, ATTN_HEAD_DIM - ROT_DIM), F32)
    zeros = jnp.zeros((seq, ATTN_HEAD_DIM - ROT_DIM), F32)
    zh = jnp.zeros((seq, half), F32)
    c_head = jnp.concatenate([cos, cos, ones], axis=1)
    s_dn_head = jnp.concatenate([zh, sin, zeros], axis=1)
    s_up_head = jnp.concatenate([-sin, zh, zeros], axis=1)
    tile = lambda a: jnp.concatenate([a, a], axis=1)
    return tile(c_head), tile(s_dn_head), tile(s_up_head)


def _attn_geometry(dilation):
    per_class = ATTN_SUPER // dilation
    rows = min(per_class, ATTN_BLOCK)
    return per_class, rows, per_class // rows


def _attn_prep(x_ref, g_ref, c_ref, sd_ref, su_ref, bd_ref, dst_ref):
    half = ROT_DIM // 2
    for pair in range(ATTN_CFG_DIM // LANES):
        ls = slice(pair * LANES, (pair + 1) * LANES)
        for r0 in range(0, ATTN_SUPER, ATTN_PREP_ROWS):
            rr = slice(r0, r0 + ATTN_PREP_ROWS)
            x = x_ref[rr, ls].astype(F32)
            ms = _dot((x * x).astype(BF16), bd_ref[...])
            y = x * lax.rsqrt(ms + EPS) * g_ref[:, ls]
            dst_ref[pair, rr, :] = (y * c_ref[rr, :] + pltpu.roll(y, half, 1) * sd_ref[rr, :]
                                    + pltpu.roll(y, LANES - half, 1) * su_ref[rr, :])


def _attn_band(rows):
    blk = ATTN_BLOCK
    qi = lax.broadcasted_iota(jnp.int32, (2 * rows, blk + rows), 0) % rows
    key_col = lax.broadcasted_iota(jnp.int32, (2 * rows, blk + rows), 1)
    dist = qi - (key_col - blk)
    band_bias = jnp.where((dist >= 0) & (dist <= ATTN_SPAN), 0.0, NEG_BIG).astype(F32)
    return band_bias, key_col


def _attn_units(dilation, units, step, band_bias, key_col, qd_ref, kd_ref, vd_ref, od_ref, ld_ref):
    per_class, rows, _ = _attn_geometry(dilation)
    blk = ATTN_BLOCK
    head0 = lax.broadcasted_iota(jnp.int32, (1, LANES), 1) < ATTN_HEAD_DIM
    items = []
    for r, j in units:
        q_rows = pl.ds(pl.multiple_of(r * per_class + j * blk, rows), rows)
        k_rows = pl.ds(pl.multiple_of(j * blk, blk), blk + rows)
        first_key = jnp.maximum(blk - (step * per_class + j * blk), 0)
        bias = jnp.where(key_col >= first_key, band_bias, NEG_BIG)
        for pair in range(ATTN_CFG_DIM // LANES):
            items.append((pair, r, q_rows, k_rows, bias))
    scores = []
    for pair, r, q_rows, k_rows, bias in items:
        qp = qd_ref[pair, q_rows, :]
        zero = jnp.zeros_like(qp)
        q2 = jnp.concatenate([jnp.where(head0, qp, zero), jnp.where(head0, zero, qp)], axis=0)
        scores.append(_dot_nt(q2, kd_ref[pair, r, k_rows, :]) + bias)
    probs = []
    for s in scores:
        mx = jnp.max(s, axis=-1, keepdims=True)
        p = jnp.exp2(s - mx)
        den = jnp.sum(p, axis=-1, keepdims=True)
        probs.append((p.astype(BF16), 1.0 / den, mx + jnp.log2(den)))
    outs = [_dot(p, vd_ref[pair, r, k_rows, :]) * inv
            for (p, inv, _), (pair, r, _, k_rows, _) in zip(probs, items)]
    for o2, (_, _, lse), (pair, _, q_rows, _, _) in zip(outs, probs, items):
        od_ref[pair, q_rows, :] = jnp.where(head0, o2[:rows], o2[rows:])
        ld_ref[pair, q_rows, :] = jnp.where(head0, lse[:rows], lse[rows:])


def _attn_config(cfg, dilation, step, ks_ref, v_ref, vs_ref, qs_ref, qd_ref, od_ref, ld_ref,
                 kd_ref, vd_ref, oacc_ref, lacc_ref):
    per_class, rows, n_blk = _attn_geometry(dilation)
    blk = ATTN_BLOCK
    pairs = ATTN_CFG_DIM // LANES

    @pl.when(step == 0)
    def _():
        zeros = jnp.zeros((pairs, dilation, blk, LANES), BF16)
        kd_ref[:, :, 0:blk, :] = zeros
        vd_ref[:, :, 0:blk, :] = zeros

    cur = slice(blk, blk + per_class)
    for pair in range(pairs):
        ls = slice(pair * LANES, (pair + 1) * LANES)
        if dilation == 1:
            qd_ref[pair] = qs_ref[pair].astype(BF16)
            kd_ref[pair, 0, cur, :] = ks_ref[pair].astype(BF16)
            vd_ref[pair, 0, cur, :] = v_ref[:, ls]
        else:
            vs_ref[pair] = v_ref[:, ls].astype(F32)
            for r in range(dilation):
                src = pl.ds(r, per_class, stride=dilation)
                qd_ref[pair, r * per_class:(r + 1) * per_class, :] = (
                    qs_ref[pair, src, :].astype(BF16))
                kd_ref[pair, r, cur, :] = ks_ref[pair, src, :].astype(BF16)
                vd_ref[pair, r, cur, :] = vs_ref[pair, src, :].astype(BF16)

    band_bias, key_col = _attn_band(rows)

    def group(gi, carry):
        units = [((gi * ATTN_GROUP + i) % dilation, (gi * ATTN_GROUP + i) // dilation)
                 for i in range(ATTN_GROUP)]
        _attn_units(dilation, units, step, band_bias, key_col,
                    qd_ref, kd_ref, vd_ref, od_ref, ld_ref)
        return carry

    lax.fori_loop(0, dilation * n_blk // ATTN_GROUP, group, 0)

    for pair in range(pairs):
        for r in range(dilation):
            src = slice(r * per_class, (r + 1) * per_class)
            dst = pl.ds(r, per_class, stride=dilation) if dilation > 1 else slice(0, per_class)
            oacc_ref[cfg, pair, dst, :] = od_ref[pair, src, :]
            lacc_ref[cfg, pair, dst, :] = ld_ref[pair, src, :]

    for ref in (kd_ref, vd_ref):
        ref[:, :, 0:blk, :] = ref[:, :, per_class:per_class + blk, :]


def _dil_attn_body(q_ref, k_ref, v_ref, qg_ref, kg_ref, c_ref, sd_ref, su_ref, bd_ref,
                   o_ref, qs_ref, ks_ref, vs_ref, od_ref, ld_ref, qd_ref,
                   kd0, kd1, kd2, vd0, vd1, vd2, oacc_ref, lacc_ref):
    step = pl.program_id(1)
    cfg_id = pl.program_id(2)
    n_cfg = len(DIL_CONFIGS)
    _attn_prep(q_ref, qg_ref, c_ref, sd_ref, su_ref, bd_ref, qs_ref)
    _attn_prep(k_ref, kg_ref, c_ref, sd_ref, su_ref, bd_ref, ks_ref)
    kds, vds = (kd0, kd1, kd2), (vd0, vd1, vd2)
    for cfg, (_, dilation) in enumerate(DIL_CONFIGS):
        @pl.when(cfg_id == cfg)
        def _(cfg=cfg, dilation=dilation):
            _attn_config(cfg, dilation, step, ks_ref, v_ref, vs_ref, qs_ref, qd_ref, od_ref,
                         ld_ref, kds[cfg], vds[cfg], oacc_ref, lacc_ref)

    @pl.when(cfg_id == n_cfg - 1)
    def _():
        for pair in range(ATTN_CFG_DIM // LANES):
            for r0 in range(0, ATTN_SUPER, ATTN_PREP_ROWS):
                rr = slice(r0, r0 + ATTN_PREP_ROWS)
                ls = [lacc_ref[c, pair, rr, :] for c in range(n_cfg)]
                mx = jnp.maximum(jnp.maximum(ls[0], ls[1]), ls[2])
                es = [jnp.exp2(l - mx) for l in ls]
                inv = 1.0 / (es[0] + es[1] + es[2])
                for c in range(n_cfg):
                    col = c * ATTN_CFG_DIM + pair * LANES
                    o_ref[rr, col:col + LANES] = (
                        oacc_ref[c, pair, rr, :] * (es[c] * inv)).astype(o_ref.dtype)


def _dilated_attention(p, qn_g, kn_g, batch, seq):
    m = p.shape[0]
    n_cfg = len(DIL_CONFIGS)
    n_sb = seq // ATTN_SUPER
    pairs = ATTN_CFG_DIM // LANES
    c_t, sd_t, su_t = _rope_tables(seq)
    head = np.arange(LANES) // ATTN_HEAD_DIM
    bd = jnp.asarray((head[:, None] == head[None, :]).astype(np.float32) / ATTN_HEAD_DIM, BF16)
    tile_g = lambda g: jnp.tile(g, ATTN_CFG_DIM // ATTN_HEAD_DIM).reshape(1, ATTN_CFG_DIM)
    qg = tile_g(qn_g) * np.float32(ATTN_HEAD_DIM ** -0.5 * np.log2(np.e))
    q0 = 2 * SGU_DIM // ATTN_CFG_DIM
    k0 = q0 + n_cfg
    v0 = k0 + n_cfg
    wide = lambda c0: pl.BlockSpec((ATTN_SUPER, ATTN_CFG_DIM),
                                   lambda b, s, c: (b * n_sb + s, c0 + c))
    tab = pl.BlockSpec((ATTN_SUPER, LANES), lambda b, s, c: (s, 0))
    hist = lambda d: pltpu.VMEM((pairs, d, ATTN_BLOCK + ATTN_SUPER // d, LANES), BF16)
    dils = [d for _, d in DIL_CONFIGS]
    return pl.pallas_call(
        _dil_attn_body,
        grid=(batch, n_sb, n_cfg),
        in_specs=[wide(q0), wide(k0), wide(v0),
                  _const_spec((1, ATTN_CFG_DIM)), _const_spec((1, ATTN_CFG_DIM)),
                  tab, tab, tab, _const_spec((LANES, LANES))],
        out_specs=pl.BlockSpec((ATTN_SUPER, ATTN_DIM), lambda b, s, c: (b * n_sb + s, 0)),
        out_shape=jax.ShapeDtypeStruct((m, ATTN_DIM), BF16),
        scratch_shapes=[pltpu.VMEM((pairs, ATTN_SUPER, LANES), F32)] * 5
        + [pltpu.VMEM((pairs, ATTN_SUPER, LANES), BF16)]
        + [hist(d) for d in dils] + [hist(d) for d in dils]
        + [pltpu.VMEM((n_cfg, pairs, ATTN_SUPER, LANES), F32)] * 2,
        compiler_params=_cparams(("parallel", "arbitrary", "arbitrary")),
        name="dil_attn",
    )(p, p, p, qg, tile_g(kn_g), c_t, sd_t, su_t, bd)


def _mem_kv_body(mem_ref, g_ref, w_ref, kg_ref, k_ref, v_ref):
    d = D_MODEL
    h = _rms(mem_ref[...], g_ref[...]).astype(BF16)
    for hh in range(XATTN_HEADS):
        ls = slice(hh * XATTN_HEAD_DIM, (hh + 1) * XATTN_HEAD_DIM)
        kh = _dot(h, w_ref[:, ls])
        k_ref[:, ls] = _rms(kh, kg_ref[...]).astype(k_ref.dtype)
    v_ref[...] = _dot(h, w_ref[:, d:]).astype(v_ref.dtype)


def _mem_kv(mem, g, wkv, kn_g, batch):
    d = D_MODEL
    spec = pl.BlockSpec((None, MEM_LEN, d), lambda b: (b, 0, 0))
    return pl.pallas_call(
        _mem_kv_body,
        grid=(batch,),
        in_specs=[spec, _const_spec((1, d)), wkv[0], _const_spec((1, XATTN_HEAD_DIM))],
        out_specs=[spec, spec],
        out_shape=[jax.ShapeDtypeStruct((batch, MEM_LEN, d), BF16)] * 2,
        compiler_params=_cparams(("parallel",)),
        name="mem_kv",
    )(mem, g.reshape(1, d), wkv[1], kn_g.reshape(1, XATTN_HEAD_DIM))


def _xattn_body(x_ref, m1_ref, m2_ref, wm_ref, gx_ref, wq_ref, qg_ref, k_ref, v_ref,
                wo_ref, o_ref):
    n1 = m1_ref.shape[1]
    x1 = (x_ref[...] + _dot(m1_ref[...], wm_ref[0:n1, :])
          + _dot(m2_ref[...], wm_ref[n1:, :]))
    q = _dot(_rms(x1, gx_ref[...]).astype(BF16), wq_ref[...])
    lanes = [slice(hh * XATTN_HEAD_DIM, (hh + 1) * XATTN_HEAD_DIM) for hh in range(XATTN_HEADS)]
    scores = [_dot_nt(_rms(q[:, ls], qg_ref[...]).astype(BF16), k_ref[:, ls])
              for ls in lanes]
    probs = []
    for s in scores:
        p = jnp.exp(s - jnp.max(s, axis=-1, keepdims=True))
        probs.append((p.astype(BF16), 1.0 / jnp.sum(p, axis=-1, keepdims=True)))
    heads = [(_dot(p, v_ref[:, ls]) * inv).astype(BF16) for (p, inv), ls in zip(probs, lanes)]
    o_ref[...] = x1 + _dot(jnp.concatenate(heads, axis=1), wo_ref[...])


def _xattn(x2d, m1, m2, w_mix, gx, wq, qn_g, k_mem, v_mem, wo, seq, tm=ROW_TILE):
    m, d = x2d.shape
    per_b = seq // tm
    row = lambda n: pl.BlockSpec((tm, n), lambda i: (i, 0))
    mem = pl.BlockSpec((None, MEM_LEN, d), lambda i: (i // per_b, 0, 0))
    qg = (qn_g * np.float32(XATTN_HEAD_DIM ** -0.5)).reshape(1, XATTN_HEAD_DIM)
    return pl.pallas_call(
        _xattn_body,
        grid=(m // tm,),
        in_specs=[row(d), row(m1.shape[1]), row(m2.shape[1]),
                  w_mix[0], _const_spec((1, d)),
                  wq[0], _const_spec((1, XATTN_HEAD_DIM)), mem, mem, wo[0]],
        out_specs=row(d),
        out_shape=jax.ShapeDtypeStruct((m, d), F32),
        compiler_params=_cparams(("parallel",)),
        name="xattn",
    )(x2d, m1, m2, w_mix[1], gx.reshape(1, d), wq[1], qg, k_mem, v_mem, wo[1])


def _mlp_body(x_ref, g_ref, w1_ref, w2_ref, o_ref, *, ff_tile):
    x = x_ref[...]
    h = _rms(x, g_ref[...]).astype(BF16)
    acc = x
    for c0 in range(0, D_FF, ff_tile):
        hid = jnp.maximum(_dot(h, w1_ref[:, c0:c0 + ff_tile]), 0.0)
        acc = acc + _dot((hid * hid).astype(BF16), w2_ref[c0:c0 + ff_tile, :])
    o_ref[...] = acc


def _mlp(x2d, g, w1, w2, tm=ROW_TILE, ff_tile=1024):
    m, d = x2d.shape
    row = pl.BlockSpec((tm, d), lambda i: (i, 0))
    return pl.pallas_call(
        functools.partial(_mlp_body, ff_tile=ff_tile),
        grid=(m // tm,),
        in_specs=[row, _const_spec((1, d)), w1[0], w2[0]],
        out_specs=row,
        out_shape=jax.ShapeDtypeStruct((m, d), F32),
        compiler_params=_cparams(("parallel",)),
        name="mlp",
    )(x2d, g.reshape(1, d), w1[1], w2[1])


def kernel(x, mem, norm_mix_g, ev_w_in, conv_dw_w, conv_dw_b, conv_ln_g, conv_ln_b,
           hgrn_lb_logits, hgrn_onorm_g, ev_w_out, od_w_in, sgu_ln_g, sgu_ln_b, sgu_w, sgu_b,
           attn_qnorm_g, attn_knorm_g, od_w_out, norm_xattn_g, norm_mem_g, xattn_wq, xattn_wkv,
           xattn_qnorm_g, xattn_knorm_g, xattn_wo, norm_mlp_g, mlp_w1, mlp_w2):
    batch, seq, d = x.shape
    (ev_w_in, ev_w_out, od_w_in, od_w_out, xattn_wq, xattn_wkv, xattn_wo, mlp_w1, mlp_w2) = (
        a.astype(BF16) for a in (ev_w_in, ev_w_out, od_w_in, od_w_out, xattn_wq, xattn_wkv,
                                 xattn_wo, mlp_w1, mlp_w2))
    lb_all = jnp.cumsum(jax.nn.softmax(hgrn_lb_logits.astype(F32), axis=0), axis=0)
    lb_all = lb_all - lb_all[0]
    xf = x.reshape(batch * seq, d)
    for l in range(DEPTH):
        if l % 2 == 0:
            e = l // 2
            m1, qig, f_pre = _even_proj(xf, norm_mix_g[l], _layer_weight(ev_w_in, e),
                                        conv_dw_w[e], conv_dw_b[e], conv_ln_g[e], conv_ln_b[e],
                                        batch, seq)
            m2 = _hgrn2(qig, f_pre, lb_all[e], hgrn_onorm_g[e], batch, seq)
            w_out = _layer_weight(ev_w_out, e)
        else:
            o = l // 2
            (p,) = _norm_proj(xf, norm_mix_g[l], _layer_weight(od_w_in, o), ODD_OUTS, ODD_PIECES)
            m1 = _chunked_sgu(p, sgu_ln_g[o], sgu_ln_b[o], sgu_w[o], sgu_b[o])
            m2 = _dilated_attention(p, attn_qnorm_g[o], attn_knorm_g[o], batch, seq)
            w_out = _layer_weight(od_w_out, o)
        k_mem, v_mem = _mem_kv(mem, norm_mem_g[l], _layer_weight(xattn_wkv, l),
                               xattn_knorm_g[l], batch)
        xf = _xattn(xf, m1, m2, w_out, norm_xattn_g[l], _layer_weight(xattn_wq, l),
                    xattn_qnorm_g[l], k_mem, v_mem, _layer_weight(xattn_wo, l), seq)
        xf = _mlp(xf, norm_mlp_g[l], _layer_weight(mlp_w1, l), _layer_weight(mlp_w2, l))
    return xf.reshape(batch, seq, d)
```

```python
import functools

import numpy as np
import jax
import jax.numpy as jnp
from jax import lax
from jax.experimental import pallas as pl
from jax.experimental.pallas import tpu as pltpu

F32 = jnp.float32
BF16 = jnp.bfloat16

D_MODEL = 1024
DEPTH = 4
EPS = 1e-6
CONV_DIM = 512
CONV_WIDTH = 31
HGRN_DIM = 512
HGRN_HEAD_DIM = 128
HGRN_HEADS = 4
HGRN_CHUNK = 64
SGU_DIM = 512
SGU_GROUPS = 4
SGU_CHUNK = 128
ATTN_HEAD_DIM = 64
DIL_CONFIGS = ((128, 1), (512, 4), (2048, 16))
ATTN_HEADS_PER_CFG = 4
ATTN_CFG_DIM = ATTN_HEADS_PER_CFG * ATTN_HEAD_DIM
ATTN_DIM = ATTN_CFG_DIM * len(DIL_CONFIGS)
ATTN_BLOCK = 128
ATTN_SPAN = 128
ROPE_THETA = 500000.0
ROT_DIM = 16
MEM_LEN = 256
XATTN_HEADS = 4
XATTN_HEAD_DIM = 256
D_FF = 4096

LANES = 128
SUBLANES = 8
ROW_TILE = 1024
HGRN_ROW_TILE = 1024
SGU_ROW_TILE = 1024
MEM_BATCH_TILE = 4
MLP_FF_TILE = 1024
CONV_HALO = 32
ATTN_SUPER = 1024
ATTN_PREP_ROWS = 256
ATTN_GROUP = 8
NEG_BIG = -1e30
VMEM_LIMIT = 56 * 1024 * 1024


def _cparams(sem):
    return pltpu.CompilerParams(dimension_semantics=sem, vmem_limit_bytes=VMEM_LIMIT)


def _const_spec(shape):
    nd = len(shape)
    return pl.BlockSpec(shape, lambda *_: (0,) * nd, pipeline_mode=pl.Buffered(1))


def _layer_weight(stack, layer):
    shape = stack.shape[1:]
    index = (layer,) + (0,) * len(shape)
    spec = pl.BlockSpec((None,) + shape, lambda *_: index, pipeline_mode=pl.Buffered(1))
    return spec, stack


def _dot(a, b):
    return jnp.dot(a, b, preferred_element_type=F32)


def _dot_nt(a, b):
    return lax.dot_general(a, b, (((1,), (1,)), ((), ())), preferred_element_type=F32)


def _dot_tn(a, b):
    return lax.dot_general(a, b, (((0,), (0,)), ((), ())), preferred_element_type=F32)


def _rms(x, g):
    return x * lax.rsqrt(jnp.mean(x * x, axis=-1, keepdims=True) + EPS) * g


def _layernorm(x, g, b):
    mu = jnp.mean(x, axis=-1, keepdims=True)
    xc = x - mu
    return xc * lax.rsqrt(jnp.mean(xc * xc, axis=-1, keepdims=True) + EPS) * g + b


def _sigmoid(x):
    return 1.0 / (1.0 + jnp.exp(-x))


def _norm_proj_body(x_ref, g_ref, w_ref, *o_refs, pieces):
    h = _rms(x_ref[...], g_ref[...]).astype(BF16)
    for out, w0, o0, n in pieces:
        o_refs[out][:, o0:o0 + n] = _dot(h, w_ref[:, w0:w0 + n]).astype(o_refs[out].dtype)


def _norm_proj(x2d, g, w, outs, pieces, tm=ROW_TILE):
    m, d = x2d.shape
    return pl.pallas_call(
        functools.partial(_norm_proj_body, pieces=pieces),
        grid=(m // tm,),
        in_specs=[pl.BlockSpec((tm, d), lambda i: (i, 0)),
                  _const_spec((1, d)), w[0]],
        out_specs=[pl.BlockSpec((tm, n), lambda i: (i, 0)) for n, _ in outs],
        out_shape=[jax.ShapeDtypeStruct((m, n), dt) for n, dt in outs],
        compiler_params=_cparams(("parallel",)),
        name="norm_proj",
    )(x2d, g.reshape(1, d), w[1])


ODD_IN = 2 * SGU_DIM + 3 * ATTN_DIM
ODD_OUTS = ((ODD_IN, BF16),)
ODD_PIECES = ((0, 0, 0, 2 * SGU_DIM),) + tuple(
    (0, 2 * SGU_DIM + i * ATTN_DIM, 2 * SGU_DIM + i * ATTN_DIM, ATTN_DIM) for i in range(3))


CONV_ROWS = 64
EVEN_ROW_TILE = 1024


def _even_proj_body(x_ref, g_ref, w_ref, cw_ref, cb_ref, lg_ref, lb_ref,
                    a_ref, qig_ref, f_ref, hbuf, sh_ref, *, tm):
    c = CONV_DIM
    hd = HGRN_DIM
    n_rows = CONV_HALO + tm
    h = _rms(x_ref[...], g_ref[...]).astype(BF16)
    a_in = _dot(h, w_ref[:, 0:2 * c])
    c0 = 2 * c
    qig_ref[:, 0:hd] = _dot(h, w_ref[:, c0:c0 + hd]).astype(BF16)
    f_ref[...] = _dot(h, w_ref[:, c0 + hd:c0 + 2 * hd])
    qig_ref[:, hd:3 * hd] = _dot(h, w_ref[:, c0 + 2 * hd:c0 + 4 * hd]).astype(BF16)

    @pl.when(pl.program_id(1) == 0)
    def _():
        hbuf[0:CONV_HALO, :] = jnp.zeros((CONV_HALO, c), F32)

    hbuf[CONV_HALO:n_rows, :] = a_in[:, :c] * _sigmoid(a_in[:, c:])
    for s in range(1, SUBLANES):
        sh_ref[s, 0:n_rows - SUBLANES, :] = hbuf[s:s + n_rows - SUBLANES, :]
    base = CONV_HALO - (CONV_WIDTH - 1)
    for r0 in range(0, tm, CONV_ROWS):
        acc = jnp.zeros((CONV_ROWS, c), F32) + cb_ref[...]
        for j in range(CONV_WIDTH):
            s = (base + j) % SUBLANES
            a0 = base + j - s + r0
            tap = hbuf[a0:a0 + CONV_ROWS, :] if s == 0 else sh_ref[s, a0:a0 + CONV_ROWS, :]
            acc = acc + cw_ref[j:j + 1, :] * tap
        y = _layernorm(acc, lg_ref[...], lb_ref[...])
        a_ref[r0:r0 + CONV_ROWS, :] = (y * _sigmoid(y)).astype(a_ref.dtype)
    hbuf[0:CONV_HALO, :] = hbuf[tm:n_rows, :]


def _even_proj(x2d, g, w, dw_w, dw_b, ln_g, ln_b, batch, seq, tm=EVEN_ROW_TILE):
    m, d = x2d.shape
    c = CONV_DIM
    hd = HGRN_DIM
    w_pad = jnp.zeros((CONV_HALO, c), F32).at[:CONV_WIDTH].set(dw_w)
    per_b = seq // tm
    row = lambda n: pl.BlockSpec((tm, n), lambda b, t: (b * per_b + t, 0))
    return pl.pallas_call(
        functools.partial(_even_proj_body, tm=tm),
        grid=(batch, per_b),
        in_specs=[row(d), _const_spec((1, d)), w[0],
                  _const_spec((CONV_HALO, c)), _const_spec((1, c)),
                  _const_spec((1, c)), _const_spec((1, c))],
        out_specs=[row(c), row(3 * hd), row(hd)],
        out_shape=[jax.ShapeDtypeStruct((m, c), BF16), jax.ShapeDtypeStruct((m, 3 * hd), BF16),
                   jax.ShapeDtypeStruct((m, hd), F32)],
        scratch_shapes=[pltpu.VMEM((CONV_HALO + tm, c), F32),
                        pltpu.VMEM((SUBLANES, CONV_HALO + tm, c), F32)],
        compiler_params=_cparams(("parallel", "arbitrary")),
        name="even_proj",
    )(x2d, g.reshape(1, d), w[1], w_pad, dw_b.reshape(1, c), ln_g.reshape(1, c),
      ln_b.reshape(1, c))


_HGRN_LEVELS = (32, 16, 8, 4, 2, 1)
_HGRN_MATRIX_LEVELS = (4, 2)
HGRN_GROUP = 4


def _hgrn_tables():
    n = HGRN_CHUNK
    t = np.arange(n)[:, None]
    u = np.arange(n)[None, :]
    blocks = [u <= t, u > t]
    masks = []
    offsets = {}
    row = 2 * n
    for m in _HGRN_LEVELS:
        c = (t // (2 * m)) * (2 * m)
        upper = (t - c) >= m
        if m in _HGRN_MATRIX_LEVELS:
            blocks.append(upper & (u >= c + m) & (u <= t))
            blocks.append((~upper) & (u >= t + 1) & (u <= c + m - 1))
            offsets[m] = (row, row + n)
            row += 2 * n
        cs = (u // (2 * m)) * (2 * m)
        masks.append(((t // (2 * m)) == (u // (2 * m))) & upper & ((u - cs) < m))
    d = np.concatenate(blocks, axis=0).astype(np.float32)
    return np.concatenate([d, d, d], axis=1), np.stack(masks).astype(np.float32), offsets


_HGRN_OFFSETS = _hgrn_tables()[2]


def _hgrn_body(q_ref, f_ref, i_ref, g_ref, lb_ref, og_ref, d_ref, m_ref, o_ref, st_ref,
               *, n_chunks):
    n = HGRN_CHUNK
    hd = HGRN_HEAD_DIM

    @pl.when(pl.program_id(1) == 0)
    def _():
        st_ref[...] = jnp.zeros(st_ref.shape, F32)

    lb = lb_ref[...]
    half = n // 2

    def level_operands(qh, kh, eh, li, m):
        if m == 1:
            return (qh * (1.0 - kh)).astype(BF16), kh.astype(BF16)
        if m in _HGRN_MATRIX_LEVELS:
            q0, k0 = _HGRN_OFFSETS[m]
            return ((qh * jnp.exp2(eh[q0:q0 + n])).astype(BF16),
                    (kh * jnp.exp2(eh[k0:k0 + n])).astype(BF16))
        e_b = eh[0:n]
        q_parts, k_parts = [], []
        for b in range(half // m):
            lo = slice(2 * b * m, (2 * b + 1) * m)
            up = slice((2 * b + 1) * m, (2 * b + 2) * m)
            ref = e_b[(2 * b + 1) * m - 1:(2 * b + 1) * m, :]
            q_parts += [qh[lo], qh[up] * jnp.exp2(e_b[up] - ref)]
            k_parts += [kh[lo] * jnp.exp2(ref - e_b[lo]), kh[up]]
        return (jnp.concatenate(q_parts, axis=0).astype(BF16),
                jnp.concatenate(k_parts, axis=0).astype(BF16))

    def group(gi, carry):
        chunks = []
        for ci in range(HGRN_GROUP):
            rows = pl.ds(pl.multiple_of((gi * HGRN_GROUP + ci) * n, n), n)
            f = lb + (1.0 - lb) * _sigmoid(f_ref[rows, :])
            logf = jnp.log2(f)
            hi = logf.astype(BF16)
            r1 = logf - hi.astype(F32)
            mid = r1.astype(BF16)
            lo = (r1 - mid.astype(F32)).astype(BF16)
            expo = _dot(d_ref[...], jnp.concatenate([hi, mid, lo], axis=0))
            chunks.append((rows, 1.0 - f, expo, q_ref[rows, :].astype(F32), i_ref[rows, :],
                           g_ref[rows, :].astype(F32)))
        items = []
        for rows, kk, expo, q, v, g in chunks:
            for h in range(HGRN_HEADS):
                ls = slice(h * hd, (h + 1) * hd)
                items.append((rows, ls, h, q[:, ls], kk[:, ls], v[:, ls], g[:, ls], expo[:, ls]))
        operands = [[level_operands(qh, kh, eh, li, m) for li, m in enumerate(_HGRN_LEVELS)]
                    for _, _, _, qh, kh, _, _, eh in items]
        q_inter = [(qh * jnp.exp2(eh[0:n])).astype(BF16) for _, _, _, qh, _, _, _, eh in items]
        k_state = [(kh * jnp.exp2(eh[n:2 * n])).astype(BF16) for _, _, _, _, kh, _, _, eh in items]
        scores = []
        for ops in operands:
            sc = jnp.zeros((n, n), F32)
            for li, (qs, ks) in enumerate(ops):
                sc = sc + m_ref[li] * _dot_nt(qs, ks)
            scores.append(sc.astype(BF16))
        intra = [_dot(sc, it[5]) for sc, it in zip(scores, items)]
        kv = [_dot_tn(it[5], ks) for ks, it in zip(k_state, items)]
        inter = []
        for idx, (_, _, h, _, _, _, _, eh) in enumerate(items):
            st = st_ref[h]
            inter.append(_dot_nt(q_inter[idx], st.astype(BF16)))
            st_ref[h] = st * jnp.exp2(eh[n - 1:n, :]) + kv[idx]
        for idx, (rows, ls, _, qh, kh, vh, gh, _) in enumerate(items):
            o = inter[idx] + intra[idx]
            o = o + jnp.sum(qh * kh, axis=-1, keepdims=True) * vh.astype(F32)
            on = _rms(o, og_ref[...])
            o_ref[rows, ls] = (on * (gh * _sigmoid(gh))).astype(o_ref.dtype)
        return carry

    lax.fori_loop(0, n_chunks // HGRN_GROUP, group, 0)


def _hgrn2(qig, f_pre, lb, onorm_g, batch, seq, tt=HGRN_ROW_TILE):
    c = HGRN_DIM
    d3, masks, _ = _hgrn_tables()
    p3 = qig.reshape(batch, seq, qig.shape[1])
    f3 = f_pre.reshape(batch, seq, c)
    col = lambda j: pl.BlockSpec((None, tt, c), lambda b, t: (b, t, j))
    out = pl.pallas_call(
        functools.partial(_hgrn_body, n_chunks=tt // HGRN_CHUNK),
        grid=(batch, seq // tt),
        in_specs=[col(0), col(0), col(1), col(2),
                  _const_spec((1, c)), _const_spec((1, HGRN_HEAD_DIM)),
                  _const_spec(d3.shape), _const_spec(masks.shape)],
        out_specs=pl.BlockSpec((None, tt, c), lambda b, t: (b, t, 0)),
        out_shape=jax.ShapeDtypeStruct((batch, seq, c), BF16),
        scratch_shapes=[pltpu.VMEM((HGRN_HEADS, HGRN_HEAD_DIM, HGRN_HEAD_DIM), F32)],
        compiler_params=_cparams(("parallel", "arbitrary")),
        name="hgrn2",
    )(p3, f3, p3, p3, lb.reshape(1, c), onorm_g.reshape(1, HGRN_HEAD_DIM),
      jnp.asarray(d3, BF16), jnp.asarray(masks, F32))
    return out.reshape(batch * seq, c)


def _sgu_body(z_ref, lg_ref, lb_ref, w_ref, bias_ref, o_ref, *, tt):
    c = SGU_DIM
    gw = c // SGU_GROUPS
    z = z_ref[...].astype(F32)
    z = 0.5 * z * (1.0 + lax.erf(z * np.float32(1.0 / np.sqrt(2.0))))
    u = z[:, :c]
    v = _layernorm(z[:, c:], lg_ref[...], lb_ref[...]).astype(BF16)
    for ci in range(tt // SGU_CHUNK):
        rs = slice(ci * SGU_CHUNK, (ci + 1) * SGU_CHUNK)
        for gi in range(SGU_GROUPS):
            ls = slice(gi * gw, (gi + 1) * gw)
            mixed = _dot(w_ref[gi], v[rs, ls]) + bias_ref[:, ls]
            o_ref[rs, ls] = (u[rs, ls] * mixed).astype(o_ref.dtype)


def _chunked_sgu(p, ln_g, ln_b, w_s, b_s, tt=SGU_ROW_TILE):
    m = p.shape[0]
    c = SGU_DIM
    tril = np.tril(np.ones((SGU_CHUNK, SGU_CHUNK), dtype=bool))
    w = jnp.where(tril[None], w_s, 0.0).astype(BF16)
    bias = jnp.repeat(b_s.T, c // SGU_GROUPS, axis=1)
    return pl.pallas_call(
        functools.partial(_sgu_body, tt=tt),
        grid=(m // tt,),
        in_specs=[pl.BlockSpec((tt, 2 * c), lambda i: (i, 0)),
                  _const_spec((1, c)), _const_spec((1, c)),
                  _const_spec(w.shape), _const_spec((SGU_CHUNK, c))],
        out_specs=pl.BlockSpec((tt, c), lambda i: (i, 0)),
        out_shape=jax.ShapeDtypeStruct((m, c), BF16),
        compiler_params=_cparams(("parallel",)),
        name="chunked_sgu",
    )(p, ln_g.reshape(1, c), ln_b.reshape(1, c), w, bias)


def _rope_tables(seq):
    half = ROT_DIM // 2
    inv_freq = jnp.power(ROPE_THETA, -jnp.arange(half, dtype=F32) / half)
    ang = jnp.arange(seq, dtype=F32)[:, None] * inv_freq[None, :]
    cos, sin = jnp.cos(ang), jnp.sin(ang)
    ones = jnp.ones((seq, ATTN_HEAD_DIM - ROT_DIM), F32)
    zeros = jnp.zeros((seq, ATTN_HEAD_DIM - ROT_DIM), F32)
    zh = jnp.zeros((seq, half), F32)
    c_head = jnp.concatenate([cos, cos, ones], axis=1)
    s_dn_head = jnp.concatenate([zh, sin, zeros], axis=1)
    s_up_head = jnp.concatenate([-sin, zh, zeros], axis=1)
    tile = lambda a: jnp.concatenate([a, a], axis=1)
    return tile(c_head), tile(s_dn_head), tile(s_up_head)


def _attn_geometry(dilation):
    per_class = ATTN_SUPER // dilation
    rows = min(per_class, ATTN_BLOCK)
    return per_class, rows, per_class // rows


def _attn_prep(x_ref, g_ref, c_ref, sd_ref, su_ref, bd_ref, dst_ref):
    half = ROT_DIM // 2
    for pair in range(ATTN_CFG_DIM // LANES):
        ls = slice(pair * LANES, (pair + 1) * LANES)
        for r0 in range(0, ATTN_SUPER, ATTN_PREP_ROWS):
            rr = slice(r0, r0 + ATTN_PREP_ROWS)
            x = x_ref[rr, ls].astype(F32)
            ms = _dot((x * x).astype(BF16), bd_ref[...])
            y = x * lax.rsqrt(ms + EPS) * g_ref[:, ls]
            dst_ref[pair, rr, :] = (y * c_ref[rr, :] + pltpu.roll(y, half, 1) * sd_ref[rr, :]
                                    + pltpu.roll(y, LANES - half, 1) * su_ref[rr, :])


def _attn_band(rows):
    blk = ATTN_BLOCK
    qi = lax.broadcasted_iota(jnp.int32, (2 * rows, blk + rows), 0) % rows
    key_col = lax.broadcasted_iota(jnp.int32, (2 * rows, blk + rows), 1)
    dist = qi - (key_col - blk)
    band_bias = jnp.where((dist >= 0) & (dist <= ATTN_SPAN), 0.0, NEG_BIG).astype(F32)
    return band_bias, key_col


def _attn_units(dilation, units, step, band_bias, key_col, qd_ref, kd_ref, vd_ref, res_refs):
    per_class, rows, _ = _attn_geometry(dilation)
    blk = ATTN_BLOCK
    od_ref, md_ref, dd_ref = res_refs
    head0 = lax.broadcasted_iota(jnp.int32, (1, LANES), 1) < ATTN_HEAD_DIM
    items = []
    for r, j in units:
        q_rows = pl.ds(pl.multiple_of(r * per_class + j * blk, rows), rows)
        k_rows = pl.ds(pl.multiple_of(j * blk, blk), blk + rows)
        first_key = jnp.maximum(blk - (step * per_class + j * blk), 0)
        bias = jnp.where(key_col >= first_key, band_bias, NEG_BIG)
        for pair in range(ATTN_CFG_DIM // LANES):
            items.append((pair, r, q_rows, k_rows, bias))
    scores = []
    for pair, r, q_rows, k_rows, bias in items:
        qp = qd_ref[pair, q_rows, :]
        zero = jnp.zeros_like(qp)
        q2 = jnp.concatenate([jnp.where(head0, qp, zero), jnp.where(head0, zero, qp)], axis=0)
        scores.append(_dot_nt(q2, kd_ref[pair, r, k_rows, :]) + bias)
    probs = []
    for s in scores:
        mx = jnp.max(s, axis=-1, keepdims=True)
        p = jnp.exp2(s - mx)
        probs.append((p.astype(BF16), mx, jnp.sum(p, axis=-1, keepdims=True)))
    outs = [_dot(p, vd_ref[pair, r, k_rows, :])
            for (p, _, _), (pair, r, _, k_rows, _) in zip(probs, items)]
    for o2, (_, mx, den), (pair, _, q_rows, _, _) in zip(outs, probs, items):
        od_ref[pair, q_rows, :] = jnp.where(head0, o2[:rows], o2[rows:])
        md_ref[pair, q_rows, :] = jnp.where(head0, mx[:rows], mx[rows:])
        dd_ref[pair, q_rows, :] = jnp.where(head0, den[:rows], den[rows:])


def _attn_config(cfg, dilation, step, ks_ref, v_ref, vs_ref, qs_ref, qd_ref, res_refs,
                 kd_ref, vd_ref, acc_refs):
    per_class, rows, n_blk = _attn_geometry(dilation)
    blk = ATTN_BLOCK
    pairs = ATTN_CFG_DIM // LANES

    @pl.when(step == 0)
    def _():
        zeros = jnp.zeros((pairs, dilation, blk, LANES), BF16)
        kd_ref[:, :, 0:blk, :] = zeros
        vd_ref[:, :, 0:blk, :] = zeros

    cur = slice(blk, blk + per_class)
    for pair in range(pairs):
        ls = slice(pair * LANES, (pair + 1) * LANES)
        if dilation == 1:
            qd_ref[pair] = qs_ref[pair].astype(BF16)
            kd_ref[pair, 0, cur, :] = ks_ref[pair].astype(BF16)
            vd_ref[pair, 0, cur, :] = v_ref[:, ls]
        else:
            vs_ref[pair] = v_ref[:, ls].astype(F32)
            for r in range(dilation):
                src = pl.ds(r, per_class, stride=dilation)
                qd_ref[pair, r * per_class:(r + 1) * per_class, :] = (
                    qs_ref[pair, src, :].astype(BF16))
                kd_ref[pair, r, cur, :] = ks_ref[pair, src, :].astype(BF16)
                vd_ref[pair, r, cur, :] = vs_ref[pair, src, :].astype(BF16)

    band_bias, key_col = _attn_band(rows)

    def group(gi, carry):
        units = [((gi * ATTN_GROUP + i) % dilation, (gi * ATTN_GROUP + i) // dilation)
                 for i in range(ATTN_GROUP)]
        _attn_units(dilation, units, step, band_bias, key_col, qd_ref, kd_ref, vd_ref, res_refs)
        return carry

    lax.fori_loop(0, dilation * n_blk // ATTN_GROUP, group, 0)

    for pair in range(pairs):
        for r in range(dilation):
            src = slice(r * per_class, (r + 1) * per_class)
            dst = pl.ds(r, per_class, stride=dilation) if dilation > 1 else slice(0, per_class)
            for acc_ref, res_ref in zip(acc_refs, res_refs):
                acc_ref[cfg, pair, dst, :] = res_ref[pair, src, :]

    for ref in (kd_ref, vd_ref):
        ref[:, :, 0:blk, :] = ref[:, :, per_class:per_class + blk, :]


def _dil_attn_body(q_ref, k_ref, v_ref, qg_ref, kg_ref, c_ref, sd_ref, su_ref, bd_ref,
                   o_ref, qs_ref, ks_ref, vs_ref, od_ref, md_ref, dd_ref, qd_ref,
                   kd0, kd1, kd2, vd0, vd1, vd2, oacc_ref, macc_ref, dacc_ref):
    step = pl.program_id(1)
    cfg_id = pl.program_id(2)
    n_cfg = len(DIL_CONFIGS)
    _attn_prep(q_ref, qg_ref, c_ref, sd_ref, su_ref, bd_ref, qs_ref)
    _attn_prep(k_ref, kg_ref, c_ref, sd_ref, su_ref, bd_ref, ks_ref)
    kds, vds = (kd0, kd1, kd2), (vd0, vd1, vd2)
    for cfg, (_, dilation) in enumerate(DIL_CONFIGS):
        @pl.when(cfg_id == cfg)
        def _(cfg=cfg, dilation=dilation):
            _attn_config(cfg, dilation, step, ks_ref, v_ref, vs_ref, qs_ref, qd_ref,
                         (od_ref, md_ref, dd_ref), kds[cfg], vds[cfg],
                         (oacc_ref, macc_ref, dacc_ref))

    @pl.when(cfg_id == n_cfg - 1)
    def _():
        for pair in range(ATTN_CFG_DIM // LANES):
            for r0 in range(0, ATTN_SUPER, ATTN_PREP_ROWS):
                rr = slice(r0, r0 + ATTN_PREP_ROWS)
                ms = [macc_ref[c, pair, rr, :] for c in range(n_cfg)]
                top = jnp.maximum(jnp.maximum(ms[0], ms[1]), ms[2])
                ws = [jnp.exp2(mc - top) for mc in ms]
                total = sum(dacc_ref[c, pair, rr, :] * ws[c] for c in range(n_cfg))
                inv = 1.0 / total
                for c in range(n_cfg):
                    col = c * ATTN_CFG_DIM + pair * LANES
                    o_ref[rr, col:col + LANES] = (
                        oacc_ref[c, pair, rr, :] * (ws[c] * inv)).astype(o_ref.dtype)


def _dilated_attention(p, qn_g, kn_g, batch, seq):
    m = p.shape[0]
    n_cfg = len(DIL_CONFIGS)
    n_sb = seq // ATTN_SUPER
    pairs = ATTN_CFG_DIM // LANES
    c_t, sd_t, su_t = _rope_tables(seq)
    head = np.arange(LANES) // ATTN_HEAD_DIM
    bd = jnp.asarray((head[:, None] == head[None, :]).astype(np.float32) / ATTN_HEAD_DIM, BF16)
    tile_g = lambda g: jnp.tile(g, ATTN_CFG_DIM // ATTN_HEAD_DIM).reshape(1, ATTN_CFG_DIM)
    qg = tile_g(qn_g) * np.float32(ATTN_HEAD_DIM ** -0.5 * np.log2(np.e))
    q0 = 2 * SGU_DIM // ATTN_CFG_DIM
    k0 = q0 + n_cfg
    v0 = k0 + n_cfg
    wide = lambda c0: pl.BlockSpec((ATTN_SUPER, ATTN_CFG_DIM),
                                   lambda b, s, c: (b * n_sb + s, c0 + c))
    tab = pl.BlockSpec((ATTN_SUPER, LANES), lambda b, s, c: (s, 0))
    hist = lambda d: pltpu.VMEM((pairs, d, ATTN_BLOCK + ATTN_SUPER // d, LANES), BF16)
    dils = [d for _, d in DIL_CONFIGS]
    return pl.pallas_call(
        _dil_attn_body,
        grid=(batch, n_sb, n_cfg),
        in_specs=[wide(q0), wide(k0), wide(v0),
                  _const_spec((1, ATTN_CFG_DIM)), _const_spec((1, ATTN_CFG_DIM)),
                  tab, tab, tab, _const_spec((LANES, LANES))],
        out_specs=pl.BlockSpec((ATTN_SUPER, ATTN_DIM), lambda b, s, c: (b * n_sb + s, 0)),
        out_shape=jax.ShapeDtypeStruct((m, ATTN_DIM), BF16),
        scratch_shapes=[pltpu.VMEM((pairs, ATTN_SUPER, LANES), F32)] * 6
        + [pltpu.VMEM((pairs, ATTN_SUPER, LANES), BF16)]
        + [hist(d) for d in dils] + [hist(d) for d in dils]
        + [pltpu.VMEM((n_cfg, pairs, ATTN_SUPER, LANES), F32)] * 3,
        compiler_params=_cparams(("parallel", "arbitrary", "arbitrary")),
        name="dil_attn",
    )(p, p, p, qg, tile_g(kn_g), c_t, sd_t, su_t, bd)


def _mem_kv_body(mem_ref, g_ref, w_ref, kg_ref, k_ref, v_ref):
    d = D_MODEL
    nb = mem_ref.shape[0]
    h = _rms(mem_ref[...].reshape(nb * MEM_LEN, d), g_ref[...]).astype(BF16)
    for hh in range(XATTN_HEADS):
        ls = slice(hh * XATTN_HEAD_DIM, (hh + 1) * XATTN_HEAD_DIM)
        kh = _rms(_dot(h, w_ref[:, ls]), kg_ref[...])
        k_ref[:, :, ls] = kh.reshape(nb, MEM_LEN, XATTN_HEAD_DIM).astype(k_ref.dtype)
    v_ref[...] = _dot(h, w_ref[:, d:]).reshape(nb, MEM_LEN, d).astype(v_ref.dtype)


def _mem_kv(mem, g, wkv, kn_g, batch, nb=MEM_BATCH_TILE):
    d = D_MODEL
    nb = int(np.gcd(batch, nb))
    spec = pl.BlockSpec((nb, MEM_LEN, d), lambda b: (b, 0, 0))
    return pl.pallas_call(
        _mem_kv_body,
        grid=(batch // nb,),
        in_specs=[spec, _const_spec((1, d)), wkv[0], _const_spec((1, XATTN_HEAD_DIM))],
        out_specs=[spec, spec],
        out_shape=[jax.ShapeDtypeStruct((batch, MEM_LEN, d), BF16)] * 2,
        compiler_params=_cparams(("parallel",)),
        name="mem_kv",
    )(mem, g.reshape(1, d), wkv[1], kn_g.reshape(1, XATTN_HEAD_DIM))


def _xattn_body(x_ref, m1_ref, m2_ref, wm_ref, gx_ref, wq_ref, qg_ref, k_ref, v_ref,
                wo_ref, o_ref):
    n1 = m1_ref.shape[1]
    x1 = (x_ref[...] + _dot(m1_ref[...], wm_ref[0:n1, :])
          + _dot(m2_ref[...], wm_ref[n1:, :]))
    q = _dot(_rms(x1, gx_ref[...]).astype(BF16), wq_ref[...])
    lanes = [slice(hh * XATTN_HEAD_DIM, (hh + 1) * XATTN_HEAD_DIM) for hh in range(XATTN_HEADS)]
    scores = [_dot_nt(_rms(q[:, ls], qg_ref[...]).astype(BF16), k_ref[:, ls])
              for ls in lanes]
    probs = []
    for s in scores:
        p = jnp.exp(s - jnp.max(s, axis=-1, keepdims=True))
        probs.append((p.astype(BF16), 1.0 / jnp.sum(p, axis=-1, keepdims=True)))
    heads = [(_dot(p, v_ref[:, ls]) * inv).astype(BF16) for (p, inv), ls in zip(probs, lanes)]
    o_ref[...] = x1 + _dot(jnp.concatenate(heads, axis=1), wo_ref[...])


def _xattn(x2d, m1, m2, w_mix, gx, wq, qn_g, k_mem, v_mem, wo, seq, tm=ROW_TILE):
    m, d = x2d.shape
    per_b = seq // tm
    row = lambda n: pl.BlockSpec((tm, n), lambda i: (i, 0))
    mem = pl.BlockSpec((None, MEM_LEN, d), lambda i: (i // per_b, 0, 0))
    qg = (qn_g * np.float32(XATTN_HEAD_DIM ** -0.5)).reshape(1, XATTN_HEAD_DIM)
    return pl.pallas_call(
        _xattn_body,
        grid=(m // tm,),
        in_specs=[row(d), row(m1.shape[1]), row(m2.shape[1]),
                  w_mix[0], _const_spec((1, d)),
                  wq[0], _const_spec((1, XATTN_HEAD_DIM)), mem, mem, wo[0]],
        out_specs=row(d),
        out_shape=jax.ShapeDtypeStruct((m, d), F32),
        compiler_params=_cparams(("parallel",)),
        name="xattn",
    )(x2d, m1, m2, w_mix[1], gx.reshape(1, d), wq[1], qg, k_mem, v_mem, wo[1])


def _mlp_body(x_ref, g_ref, w1_ref, w2_ref, o_ref, *, ff_tile):
    x = x_ref[...]
    h = _rms(x, g_ref[...]).astype(BF16)
    acc = x
    for c0 in range(0, D_FF, ff_tile):
        hid = jnp.maximum(_dot(h, w1_ref[:, c0:c0 + ff_tile]), 0.0)
        acc = acc + _dot((hid * hid).astype(BF16), w2_ref[c0:c0 + ff_tile, :])
    o_ref[...] = acc


def _mlp(x2d, g, w1, w2, tm=ROW_TILE, ff_tile=MLP_FF_TILE):
    m, d = x2d.shape
    row = pl.BlockSpec((tm, d), lambda i: (i, 0))
    return pl.pallas_call(
        functools.partial(_mlp_body, ff_tile=ff_tile),
        grid=(m // tm,),
        in_specs=[row, _const_spec((1, d)), w1[0], w2[0]],
        out_specs=row,
        out_shape=jax.ShapeDtypeStruct((m, d), F32),
        compiler_params=_cparams(("parallel",)),
        name="mlp",
    )(x2d, g.reshape(1, d), w1[1], w2[1])


def kernel(x, mem, norm_mix_g, ev_w_in, conv_dw_w, conv_dw_b, conv_ln_g, conv_ln_b,
           hgrn_lb_logits, hgrn_onorm_g, ev_w_out, od_w_in, sgu_ln_g, sgu_ln_b, sgu_w, sgu_b,
           attn_qnorm_g, attn_knorm_g, od_w_out, norm_xattn_g, norm_mem_g, xattn_wq, xattn_wkv,
           xattn_qnorm_g, xattn_knorm_g, xattn_wo, norm_mlp_g, mlp_w1, mlp_w2):
    batch, seq, d = x.shape
    (ev_w_in, ev_w_out, od_w_in, od_w_out, xattn_wq, xattn_wkv, xattn_wo, mlp_w1, mlp_w2) = (
        a.astype(BF16) for a in (ev_w_in, ev_w_out, od_w_in, od_w_out, xattn_wq, xattn_wkv,
                                 xattn_wo, mlp_w1, mlp_w2))
    lb_all = jnp.cumsum(jax.nn.softmax(hgrn_lb_logits.astype(F32), axis=0), axis=0)
    lb_all = lb_all - lb_all[0]
    xf = x.reshape(batch * seq, d)
    for l in range(DEPTH):
        if l % 2 == 0:
            e = l // 2
            m1, qig, f_pre = _even_proj(xf, norm_mix_g[l], _layer_weight(ev_w_in, e),
                                        conv_dw_w[e], conv_dw_b[e], conv_ln_g[e], conv_ln_b[e],
                                        batch, seq)
            m2 = _hgrn2(qig, f_pre, lb_all[e], hgrn_onorm_g[e], batch, seq)
            w_out = _layer_weight(ev_w_out, e)
        else:
            o = l // 2
            (p,) = _norm_proj(xf, norm_mix_g[l], _layer_weight(od_w_in, o), ODD_OUTS, ODD_PIECES)
            m1 = _chunked_sgu(p, sgu_ln_g[o], sgu_ln_b[o], sgu_w[o], sgu_b[o])
            m2 = _dilated_attention(p, attn_qnorm_g[o], attn_knorm_g[o], batch, seq)
            w_out = _layer_weight(od_w_out, o)
        k_mem, v_mem = _mem_kv(mem, norm_mem_g[l], _layer_weight(xattn_wkv, l),
                               xattn_knorm_g[l], batch)
        xf = _xattn(xf, m1, m2, w_out, norm_xattn_g[l], _layer_weight(xattn_wq, l),
                    xattn_qnorm_g[l], k_mem, v_mem, _layer_weight(xattn_wo, l), seq)
        xf = _mlp(xf, norm_mlp_g[l], _layer_weight(mlp_w1, l), _layer_weight(mlp_w2, l))
    return xf.reshape(batch, seq, d)
```

```python
import functools

import numpy as np
import jax
import jax.numpy as jnp
from jax import lax
from jax.experimental import pallas as pl
from jax.experimental.pallas import tpu as pltpu

F32 = jnp.float32
BF16 = jnp.bfloat16

D_MODEL = 1024
DEPTH = 4
EPS = 1e-6
CONV_DIM = 512
CONV_WIDTH = 31
HGRN_DIM = 512
HGRN_HEAD_DIM = 128
HGRN_HEADS = 4
HGRN_CHUNK = 64
SGU_DIM = 512
SGU_GROUPS = 4
SGU_CHUNK = 128
ATTN_HEAD_DIM = 64
DIL_CONFIGS = ((128, 1), (512, 4), (2048, 16))
ATTN_HEADS_PER_CFG = 4
ATTN_CFG_DIM = ATTN_HEADS_PER_CFG * ATTN_HEAD_DIM
ATTN_DIM = ATTN_CFG_DIM * len(DIL_CONFIGS)
ATTN_BLOCK = 128
ATTN_SPAN = 128
ROPE_THETA = 500000.0
ROT_DIM = 16
MEM_LEN = 256
XATTN_HEADS = 4
XATTN_HEAD_DIM = 256
D_FF = 4096

LANES = 128
SUBLANES = 8
ROW_TILE = 1024
HGRN_ROW_TILE = 2048
SGU_ROW_TILE = 2048
MEM_BATCH_TILE = 4
MLP_FF_TILE = 1024
CONV_HALO = 32
ATTN_SUPER = 1024
ATTN_PREP_ROWS = 256
ATTN_GROUP = 8
NEG_BIG = -1e30
VMEM_LIMIT = 56 * 1024 * 1024


def _cparams(sem):
    return pltpu.CompilerParams(dimension_semantics=sem, vmem_limit_bytes=VMEM_LIMIT)


def _const_spec(shape):
    nd = len(shape)
    return pl.BlockSpec(shape, lambda *_: (0,) * nd, pipeline_mode=pl.Buffered(1))


def _layer_weight(stack, layer):
    shape = stack.shape[1:]
    index = (layer,) + (0,) * len(shape)
    spec = pl.BlockSpec((None,) + shape, lambda *_: index, pipeline_mode=pl.Buffered(1))
    return spec, stack


def _dot(a, b):
    return jnp.dot(a, b, preferred_element_type=F32)


def _dot_nt(a, b):
    return lax.dot_general(a, b, (((1,), (1,)), ((), ())), preferred_element_type=F32)


def _dot_tn(a, b):
    return lax.dot_general(a, b, (((0,), (0,)), ((), ())), preferred_element_type=F32)


def _rms(x, g):
    return x * lax.rsqrt(jnp.mean(x * x, axis=-1, keepdims=True) + EPS) * g


def _layernorm(x, g, b):
    mu = jnp.mean(x, axis=-1, keepdims=True)
    xc = x - mu
    return xc * lax.rsqrt(jnp.mean(xc * xc, axis=-1, keepdims=True) + EPS) * g + b


def _sigmoid(x):
    return 1.0 / (1.0 + jnp.exp(-x))


def _norm_proj_body(x_ref, g_ref, w_ref, *o_refs, pieces):
    h = _rms(x_ref[...], g_ref[...]).astype(BF16)
    for out, w0, o0, n in pieces:
        o_refs[out][:, o0:o0 + n] = _dot(h, w_ref[:, w0:w0 + n]).astype(o_refs[out].dtype)


def _norm_proj(x2d, g, w, outs, pieces, tm=ROW_TILE):
    m, d = x2d.shape
    return pl.pallas_call(
        functools.partial(_norm_proj_body, pieces=pieces),
        grid=(m // tm,),
        in_specs=[pl.BlockSpec((tm, d), lambda i: (i, 0)),
                  _const_spec((1, d)), w[0]],
        out_specs=[pl.BlockSpec((tm, n), lambda i: (i, 0)) for n, _ in outs],
        out_shape=[jax.ShapeDtypeStruct((m, n), dt) for n, dt in outs],
        compiler_params=_cparams(("parallel",)),
        name="norm_proj",
    )(x2d, g.reshape(1, d), w[1])


ODD_IN = 2 * SGU_DIM + 3 * ATTN_DIM
ODD_OUTS = ((ODD_IN, BF16),)
ODD_PIECES = ((0, 0, 0, 2 * SGU_DIM),) + tuple(
    (0, 2 * SGU_DIM + i * ATTN_DIM, 2 * SGU_DIM + i * ATTN_DIM, ATTN_DIM) for i in range(3))


CONV_ROWS = 64
EVEN_ROW_TILE = 1024


def _even_proj_body(x_ref, g_ref, w_ref, cw_ref, cb_ref, lg_ref, lb_ref,
                    a_ref, qig_ref, f_ref, hbuf, sh_ref, *, tm):
    c = CONV_DIM
    hd = HGRN_DIM
    n_rows = CONV_HALO + tm
    h = _rms(x_ref[...], g_ref[...]).astype(BF16)
    a_in = _dot(h, w_ref[:, 0:2 * c])
    c0 = 2 * c
    qig_ref[:, 0:hd] = _dot(h, w_ref[:, c0:c0 + hd]).astype(BF16)
    f_ref[...] = _dot(h, w_ref[:, c0 + hd:c0 + 2 * hd])
    qig_ref[:, hd:3 * hd] = _dot(h, w_ref[:, c0 + 2 * hd:c0 + 4 * hd]).astype(BF16)

    @pl.when(pl.program_id(1) == 0)
    def _():
        hbuf[0:CONV_HALO, :] = jnp.zeros((CONV_HALO, c), F32)

    hbuf[CONV_HALO:n_rows, :] = a_in[:, :c] * _sigmoid(a_in[:, c:])
    for s in range(1, SUBLANES):
        sh_ref[s, 0:n_rows - SUBLANES, :] = hbuf[s:s + n_rows - SUBLANES, :]
    base = CONV_HALO - (CONV_WIDTH - 1)
    for r0 in range(0, tm, CONV_ROWS):
        acc = jnp.zeros((CONV_ROWS, c), F32) + cb_ref[...]
        for j in range(CONV_WIDTH):
            s = (base + j) % SUBLANES
            a0 = base + j - s + r0
            tap = hbuf[a0:a0 + CONV_ROWS, :] if s == 0 else sh_ref[s, a0:a0 + CONV_ROWS, :]
            acc = acc + cw_ref[j:j + 1, :] * tap
        y = _layernorm(acc, lg_ref[...], lb_ref[...])
        a_ref[r0:r0 + CONV_ROWS, :] = (y * _sigmoid(y)).astype(a_ref.dtype)
    hbuf[0:CONV_HALO, :] = hbuf[tm:n_rows, :]


def _even_proj(x2d, g, w, dw_w, dw_b, ln_g, ln_b, batch, seq, tm=EVEN_ROW_TILE):
    m, d = x2d.shape
    c = CONV_DIM
    hd = HGRN_DIM
    w_pad = jnp.zeros((CONV_HALO, c), F32).at[:CONV_WIDTH].set(dw_w)
    per_b = seq // tm
    row = lambda n: pl.BlockSpec((tm, n), lambda b, t: (b * per_b + t, 0))
    return pl.pallas_call(
        functools.partial(_even_proj_body, tm=tm),
        grid=(batch, per_b),
        in_specs=[row(d), _const_spec((1, d)), w[0],
                  _const_spec((CONV_HALO, c)), _const_spec((1, c)),
                  _const_spec((1, c)), _const_spec((1, c))],
        out_specs=[row(c), row(3 * hd), row(hd)],
        out_shape=[jax.ShapeDtypeStruct((m, c), BF16), jax.ShapeDtypeStruct((m, 3 * hd), BF16),
                   jax.ShapeDtypeStruct((m, hd), F32)],
        scratch_shapes=[pltpu.VMEM((CONV_HALO + tm, c), F32),
                        pltpu.VMEM((SUBLANES, CONV_HALO + tm, c), F32)],
        compiler_params=_cparams(("parallel", "arbitrary")),
        name="even_proj",
    )(x2d, g.reshape(1, d), w[1], w_pad, dw_b.reshape(1, c), ln_g.reshape(1, c),
      ln_b.reshape(1, c))


_HGRN_LEVELS = (32, 16, 8, 4, 2, 1)
_HGRN_MATRIX_LEVELS = (4, 2)
HGRN_GROUP = 4


def _hgrn_tables():
    n = HGRN_CHUNK
    t = np.arange(n)[:, None]
    u = np.arange(n)[None, :]
    blocks = [u <= t, u > t]
    masks = []
    offsets = {}
    row = 2 * n
    for m in _HGRN_LEVELS:
        c = (t // (2 * m)) * (2 * m)
        upper = (t - c) >= m
        if m in _HGRN_MATRIX_LEVELS:
            blocks.append(upper & (u >= c + m) & (u <= t))
            blocks.append((~upper) & (u >= t + 1) & (u <= c + m - 1))
            offsets[m] = (row, row + n)
            row += 2 * n
        cs = (u // (2 * m)) * (2 * m)
        masks.append(((t // (2 * m)) == (u // (2 * m))) & upper & ((u - cs) < m))
    d = np.concatenate(blocks, axis=0).astype(np.float32)
    return np.concatenate([d, d, d], axis=1), np.stack(masks).astype(np.float32), offsets


_HGRN_OFFSETS = _hgrn_tables()[2]


def _hgrn_body(q_ref, f_ref, i_ref, g_ref, lb_ref, og_ref, d_ref, m_ref, o_ref, st_ref,
               *, n_chunks):
    n = HGRN_CHUNK
    hd = HGRN_HEAD_DIM

    @pl.when(pl.program_id(1) == 0)
    def _():
        st_ref[...] = jnp.zeros(st_ref.shape, F32)

    lb = lb_ref[...]
    half = n // 2

    def level_operands(qh, kh, eh, li, m):
        if m == 1:
            return (qh * (1.0 - kh)).astype(BF16), kh.astype(BF16)
        if m in _HGRN_MATRIX_LEVELS:
            q0, k0 = _HGRN_OFFSETS[m]
            return ((qh * jnp.exp2(eh[q0:q0 + n])).astype(BF16),
                    (kh * jnp.exp2(eh[k0:k0 + n])).astype(BF16))
        e_b = eh[0:n]
        q_parts, k_parts = [], []
        for b in range(half // m):
            lo = slice(2 * b * m, (2 * b + 1) * m)
            up = slice((2 * b + 1) * m, (2 * b + 2) * m)
            ref = e_b[(2 * b + 1) * m - 1:(2 * b + 1) * m, :]
            q_parts += [qh[lo], qh[up] * jnp.exp2(e_b[up] - ref)]
            k_parts += [kh[lo] * jnp.exp2(ref - e_b[lo]), kh[up]]
        return (jnp.concatenate(q_parts, axis=0).astype(BF16),
                jnp.concatenate(k_parts, axis=0).astype(BF16))

    def group(gi, carry):
        chunks = []
        for ci in range(HGRN_GROUP):
            rows = pl.ds(pl.multiple_of((gi * HGRN_GROUP + ci) * n, n), n)
            f = lb + (1.0 - lb) * _sigmoid(f_ref[rows, :])
            logf = jnp.log2(f)
            hi = logf.astype(BF16)
            r1 = logf - hi.astype(F32)
            mid = r1.astype(BF16)
            lo = (r1 - mid.astype(F32)).astype(BF16)
            expo = _dot(d_ref[...], jnp.concatenate([hi, mid, lo], axis=0))
            chunks.append((rows, 1.0 - f, expo, q_ref[rows, :].astype(F32), i_ref[rows, :],
                           g_ref[rows, :].astype(F32)))
        items = []
        for rows, kk, expo, q, v, g in chunks:
            for h in range(HGRN_HEADS):
                ls = slice(h * hd, (h + 1) * hd)
                items.append((rows, ls, h, q[:, ls], kk[:, ls], v[:, ls], g[:, ls], expo[:, ls]))
        operands = [[level_operands(qh, kh, eh, li, m) for li, m in enumerate(_HGRN_LEVELS)]
                    for _, _, _, qh, kh, _, _, eh in items]
        q_inter = [(qh * jnp.exp2(eh[0:n])).astype(BF16) for _, _, _, qh, _, _, _, eh in items]
        k_state = [(kh * jnp.exp2(eh[n:2 * n])).astype(BF16) for _, _, _, _, kh, _, _, eh in items]
        scores = []
        for ops in operands:
            sc = jnp.zeros((n, n), F32)
            for li, (qs, ks) in enumerate(ops):
                sc = sc + m_ref[li] * _dot_nt(qs, ks)
            scores.append(sc.astype(BF16))
        intra = [_dot(sc, it[5]) for sc, it in zip(scores, items)]
        kv = [_dot_tn(it[5], ks) for ks, it in zip(k_state, items)]
        inter = []
        for idx, (_, _, h, _, _, _, _, eh) in enumerate(items):
            st = st_ref[h]
            inter.append(_dot_nt(q_inter[idx], st.astype(BF16)))
            st_ref[h] = st * jnp.exp2(eh[n - 1:n, :]) + kv[idx]
        for idx, (rows, ls, _, qh, kh, vh, gh, _) in enumerate(items):
            o = inter[idx] + intra[idx]
            o = o + jnp.sum(qh * kh, axis=-1, keepdims=True) * vh.astype(F32)
            on = _rms(o, og_ref[...])
            o_ref[rows, ls] = (on * (gh * _sigmoid(gh))).astype(o_ref.dtype)
        return carry

    lax.fori_loop(0, n_chunks // HGRN_GROUP, group, 0)


def _hgrn2(qig, f_pre, lb, onorm_g, batch, seq, tt=HGRN_ROW_TILE):
    c = HGRN_DIM
    d3, masks, _ = _hgrn_tables()
    p3 = qig.reshape(batch, seq, qig.shape[1])
    f3 = f_pre.reshape(batch, seq, c)
    col = lambda j: pl.BlockSpec((None, tt, c), lambda b, t: (b, t, j))
    out = pl.pallas_call(
        functools.partial(_hgrn_body, n_chunks=tt // HGRN_CHUNK),
        grid=(batch, seq // tt),
        in_specs=[col(0), col(0), col(1), col(2),
                  _const_spec((1, c)), _const_spec((1, HGRN_HEAD_DIM)),
                  _const_spec(d3.shape), _const_spec(masks.shape)],
        out_specs=pl.BlockSpec((None, tt, c), lambda b, t: (b, t, 0)),
        out_shape=jax.ShapeDtypeStruct((batch, seq, c), BF16),
        scratch_shapes=[pltpu.VMEM((HGRN_HEADS, HGRN_HEAD_DIM, HGRN_HEAD_DIM), F32)],
        compiler_params=_cparams(("parallel", "arbitrary")),
        name="hgrn2",
    )(p3, f3, p3, p3, lb.reshape(1, c), onorm_g.reshape(1, HGRN_HEAD_DIM),
      jnp.asarray(d3, BF16), jnp.asarray(masks, F32))
    return out.reshape(batch * seq, c)


def _sgu_body(z_ref, lg_ref, lb_ref, w_ref, bias_ref, o_ref, *, tt):
    c = SGU_DIM
    gw = c // SGU_GROUPS
    z = z_ref[...].astype(F32)
    z = 0.5 * z * (1.0 + lax.erf(z * np.float32(1.0 / np.sqrt(2.0))))
    u = z[:, :c]
    v = _layernorm(z[:, c:], lg_ref[...], lb_ref[...]).astype(BF16)
    for ci in range(tt // SGU_CHUNK):
        rs = slice(ci * SGU_CHUNK, (ci + 1) * SGU_CHUNK)
        for gi in range(SGU_GROUPS):
            ls = slice(gi * gw, (gi + 1) * gw)
            mixed = _dot(w_ref[gi], v[rs, ls]) + bias_ref[:, ls]
            o_ref[rs, ls] = (u[rs, ls] * mixed).astype(o_ref.dtype)


def _chunked_sgu(p, ln_g, ln_b, w_s, b_s, tt=SGU_ROW_TILE):
    m = p.shape[0]
    c = SGU_DIM
    tril = np.tril(np.ones((SGU_CHUNK, SGU_CHUNK), dtype=bool))
    w = jnp.where(tril[None], w_s, 0.0).astype(BF16)
    bias = jnp.repeat(b_s.T, c // SGU_GROUPS, axis=1)
    return pl.pallas_call(
        functools.partial(_sgu_body, tt=tt),
        grid=(m // tt,),
        in_specs=[pl.BlockSpec((tt, 2 * c), lambda i: (i, 0)),
                  _const_spec((1, c)), _const_spec((1, c)),
                  _const_spec(w.shape), _const_spec((SGU_CHUNK, c))],
        out_specs=pl.BlockSpec((tt, c), lambda i: (i, 0)),
        out_shape=jax.ShapeDtypeStruct((m, c), BF16),
        compiler_params=_cparams(("parallel",)),
        name="chunked_sgu",
    )(p, ln_g.reshape(1, c), ln_b.reshape(1, c), w, bias)


def _rope_tables(seq):
    half = ROT_DIM // 2
    inv_freq = jnp.power(ROPE_THETA, -jnp.arange(half, dtype=F32) / half)
    ang = jnp.arange(seq, dtype=F32)[:, None] * inv_freq[None, :]
    cos, sin = jnp.cos(ang), jnp.sin(ang)
    ones = jnp.ones((seq, ATTN_HEAD_DIM - ROT_DIM), F32)
    zeros = jnp.zeros((seq, ATTN_HEAD_DIM - ROT_DIM), F32)
    zh = jnp.zeros((seq, half), F32)
    c_head = jnp.concatenate([cos, cos, ones], axis=1)
    s_dn_head = jnp.concatenate([zh, sin, zeros], axis=1)
    s_up_head = jnp.concatenate([-sin, zh, zeros], axis=1)
    tile = lambda a: jnp.concatenate([a, a], axis=1)
    return tile(c_head), tile(s_dn_head), tile(s_up_head)


def _attn_geometry(dilation):
    per_class = ATTN_SUPER // dilation
    rows = min(per_class, ATTN_BLOCK)
    return per_class, rows, per_class // rows


def _attn_prep(x_ref, g_ref, c_ref, sd_ref, su_ref, bd_ref, dst_ref):
    half = ROT_DIM // 2
    for pair in range(ATTN_CFG_DIM // LANES):
        ls = slice(pair * LANES, (pair + 1) * LANES)
        for r0 in range(0, ATTN_SUPER, ATTN_PREP_ROWS):
            rr = slice(r0, r0 + ATTN_PREP_ROWS)
            x = x_ref[rr, ls].astype(F32)
            ms = _dot((x * x).astype(BF16), bd_ref[...])
            y = x * lax.rsqrt(ms + EPS) * g_ref[:, ls]
            dst_ref[pair, rr, :] = (y * c_ref[rr, :] + pltpu.roll(y, half, 1) * sd_ref[rr, :]
                                    + pltpu.roll(y, LANES - half, 1) * su_ref[rr, :])


def _attn_band(rows):
    blk = ATTN_BLOCK
    qi = lax.broadcasted_iota(jnp.int32, (2 * rows, blk + rows), 0) % rows
    key_col = lax.broadcasted_iota(jnp.int32, (2 * rows, blk + rows), 1)
    dist = qi - (key_col - blk)
    band_bias = jnp.where((dist >= 0) & (dist <= ATTN_SPAN), 0.0, NEG_BIG).astype(F32)
    return band_bias, key_col


def _attn_units(dilation, units, step, band_bias, key_col, qd_ref, kd_ref, vd_ref, res_refs):
    per_class, rows, _ = _attn_geometry(dilation)
    blk = ATTN_BLOCK
    od_ref, md_ref, dd_ref = res_refs
    head0 = lax.broadcasted_iota(jnp.int32, (1, LANES), 1) < ATTN_HEAD_DIM
    items = []
    for r, j in units:
        q_rows = pl.ds(pl.multiple_of(r * per_class + j * blk, rows), rows)
        k_rows = pl.ds(pl.multiple_of(j * blk, blk), blk + rows)
        first_key = jnp.maximum(blk - (step * per_class + j * blk), 0)
        bias = jnp.where(key_col >= first_key, band_bias, NEG_BIG)
        for pair in range(ATTN_CFG_DIM // LANES):
            items.append((pair, r, q_rows, k_rows, bias))
    scores = []
    for pair, r, q_rows, k_rows, bias in items:
        qp = qd_ref[pair, q_rows, :]
        zero = jnp.zeros_like(qp)
        q2 = jnp.concatenate([jnp.where(head0, qp, zero), jnp.where(head0, zero, qp)], axis=0)
        scores.append(_dot_nt(q2, kd_ref[pair, r, k_rows, :]) + bias)
    probs = []
    for s in scores:
        mx = jnp.max(s, axis=-1, keepdims=True)
        p = jnp.exp2(s - mx)
        probs.append((p.astype(BF16), mx, jnp.sum(p, axis=-1, keepdims=True)))
    outs = [_dot(p, vd_ref[pair, r, k_rows, :])
            for (p, _, _), (pair, r, _, k_rows, _) in zip(probs, items)]
    for o2, (_, mx, den), (pair, _, q_rows, _, _) in zip(outs, probs, items):
        od_ref[pair, q_rows, :] = jnp.where(head0, o2[:rows], o2[rows:])
        md_ref[pair, q_rows, :] = jnp.where(head0, mx[:rows], mx[rows:])
        dd_ref[pair, q_rows, :] = jnp.where(head0, den[:rows], den[rows:])


def _attn_config(cfg, dilation, step, ks_ref, v_ref, vs_ref, qs_ref, qd_ref, res_refs,
                 kd_ref, vd_ref, acc_refs):
    per_class, rows, n_blk = _attn_geometry(dilation)
    blk = ATTN_BLOCK
    pairs = ATTN_CFG_DIM // LANES

    @pl.when(step == 0)
    def _():
        zeros = jnp.zeros((pairs, dilation, blk, LANES), BF16)
        kd_ref[:, :, 0:blk, :] = zeros
        vd_ref[:, :, 0:blk, :] = zeros

    cur = slice(blk, blk + per_class)
    for pair in range(pairs):
        ls = slice(pair * LANES, (pair + 1) * LANES)
        if dilation == 1:
            qd_ref[pair] = qs_ref[pair].astype(BF16)
            kd_ref[pair, 0, cur, :] = ks_ref[pair].astype(BF16)
            vd_ref[pair, 0, cur, :] = v_ref[:, ls]
        else:
            vs_ref[pair] = v_ref[:, ls].astype(F32)
            for r in range(dilation):
                src = pl.ds(r, per_class, stride=dilation)
                qd_ref[pair, r * per_class:(r + 1) * per_class, :] = (
                    qs_ref[pair, src, :].astype(BF16))
                kd_ref[pair, r, cur, :] = ks_ref[pair, src, :].astype(BF16)
                vd_ref[pair, r, cur, :] = vs_ref[pair, src, :].astype(BF16)

    band_bias, key_col = _attn_band(rows)

    def group(gi, carry):
        units = [((gi * ATTN_GROUP + i) % dilation, (gi * ATTN_GROUP + i) // dilation)
                 for i in range(ATTN_GROUP)]
        _attn_units(dilation, units, step, band_bias, key_col, qd_ref, kd_ref, vd_ref, res_refs)
        return carry

    lax.fori_loop(0, dilation * n_blk // ATTN_GROUP, group, 0)

    for pair in range(pairs):
        for r in range(dilation):
            src = slice(r * per_class, (r + 1) * per_class)
            dst = pl.ds(r, per_class, stride=dilation) if dilation > 1 else slice(0, per_class)
            for acc_ref, res_ref in zip(acc_refs, res_refs):
                acc_ref[cfg, pair, dst, :] = res_ref[pair, src, :]

    for ref in (kd_ref, vd_ref):
        ref[:, :, 0:blk, :] = ref[:, :, per_class:per_class + blk, :]


def _dil_attn_body(q_ref, k_ref, v_ref, qg_ref, kg_ref, c_ref, sd_ref, su_ref, bd_ref,
                   o_ref, qs_ref, ks_ref, vs_ref, od_ref, md_ref, dd_ref, qd_ref,
                   kd0, kd1, kd2, vd0, vd1, vd2, oacc_ref, macc_ref, dacc_ref):
    step = pl.program_id(1)
    cfg_id = pl.program_id(2)
    n_cfg = len(DIL_CONFIGS)
    _attn_prep(q_ref, qg_ref, c_ref, sd_ref, su_ref, bd_ref, qs_ref)
    _attn_prep(k_ref, kg_ref, c_ref, sd_ref, su_ref, bd_ref, ks_ref)
    kds, vds = (kd0, kd1, kd2), (vd0, vd1, vd2)
    for cfg, (_, dilation) in enumerate(DIL_CONFIGS):
        @pl.when(cfg_id == cfg)
        def _(cfg=cfg, dilation=dilation):
            _attn_config(cfg, dilation, step, ks_ref, v_ref, vs_ref, qs_ref, qd_ref,
                         (od_ref, md_ref, dd_ref), kds[cfg], vds[cfg],
                         (oacc_ref, macc_ref, dacc_ref))

    @pl.when(cfg_id == n_cfg - 1)
    def _():
        for pair in range(ATTN_CFG_DIM // LANES):
            for r0 in range(0, ATTN_SUPER, ATTN_PREP_ROWS):
                rr = slice(r0, r0 + ATTN_PREP_ROWS)
                ms = [macc_ref[c, pair, rr, :] for c in range(n_cfg)]
                top = jnp.maximum(jnp.maximum(ms[0], ms[1]), ms[2])
                ws = [jnp.exp2(mc - top) for mc in ms]
                total = sum(dacc_ref[c, pair, rr, :] * ws[c] for c in range(n_cfg))
                inv = 1.0 / total
                for c in range(n_cfg):
                    col = c * ATTN_CFG_DIM + pair * LANES
                    o_ref[rr, col:col + LANES] = (
                        oacc_ref[c, pair, rr, :] * (ws[c] * inv)).astype(o_ref.dtype)


def _dilated_attention(p, qn_g, kn_g, batch, seq):
    m = p.shape[0]
    n_cfg = len(DIL_CONFIGS)
    n_sb = seq // ATTN_SUPER
    pairs = ATTN_CFG_DIM // LANES
    c_t, sd_t, su_t = _rope_tables(seq)
    head = np.arange(LANES) // ATTN_HEAD_DIM
    bd = jnp.asarray((head[:, None] == head[None, :]).astype(np.float32) / ATTN_HEAD_DIM, BF16)
    tile_g = lambda g: jnp.tile(g, ATTN_CFG_DIM // ATTN_HEAD_DIM).reshape(1, ATTN_CFG_DIM)
    qg = tile_g(qn_g) * np.float32(ATTN_HEAD_DIM ** -0.5 * np.log2(np.e))
    q0 = 2 * SGU_DIM // ATTN_CFG_DIM
    k0 = q0 + n_cfg
    v0 = k0 + n_cfg
    wide = lambda c0: pl.BlockSpec((ATTN_SUPER, ATTN_CFG_DIM),
                                   lambda b, s, c: (b * n_sb + s, c0 + c))
    tab = pl.BlockSpec((ATTN_SUPER, LANES), lambda b, s, c: (s, 0))
    hist = lambda d: pltpu.VMEM((pairs, d, ATTN_BLOCK + ATTN_SUPER // d, LANES), BF16)
    dils = [d for _, d in DIL_CONFIGS]
    return pl.pallas_call(
        _dil_attn_body,
        grid=(batch, n_sb, n_cfg),
        in_specs=[wide(q0), wide(k0), wide(v0),
                  _const_spec((1, ATTN_CFG_DIM)), _const_spec((1, ATTN_CFG_DIM)),
                  tab, tab, tab, _const_spec((LANES, LANES))],
        out_specs=pl.BlockSpec((ATTN_SUPER, ATTN_DIM), lambda b, s, c: (b * n_sb + s, 0)),
        out_shape=jax.ShapeDtypeStruct((m, ATTN_DIM), BF16),
        scratch_shapes=[pltpu.VMEM((pairs, ATTN_SUPER, LANES), F32)] * 6
        + [pltpu.VMEM((pairs, ATTN_SUPER, LANES), BF16)]
        + [hist(d) for d in dils] + [hist(d) for d in dils]
        + [pltpu.VMEM((n_cfg, pairs, ATTN_SUPER, LANES), F32)] * 3,
        compiler_params=_cparams(("parallel", "arbitrary", "arbitrary")),
        name="dil_attn",
    )(p, p, p, qg, tile_g(kn_g), c_t, sd_t, su_t, bd)


def _mem_kv_body(mem_ref, g_ref, w_ref, kg_ref, k_ref, v_ref):
    d = D_MODEL
    nb = mem_ref.shape[0]
    h = _rms(mem_ref[...].reshape(nb * MEM_LEN, d), g_ref[...]).astype(BF16)
    for hh in range(XATTN_HEADS):
        ls = slice(hh * XATTN_HEAD_DIM, (hh + 1) * XATTN_HEAD_DIM)
        kh = _rms(_dot(h, w_ref[:, ls]), kg_ref[...])
        k_ref[:, :, ls] = kh.reshape(nb, MEM_LEN, XATTN_HEAD_DIM).astype(k_ref.dtype)
    v_ref[...] = _dot(h, w_ref[:, d:]).reshape(nb, MEM_LEN, d).astype(v_ref.dtype)


def _mem_kv(mem, g, wkv, kn_g, batch, nb=MEM_BATCH_TILE):
    d = D_MODEL
    nb = int(np.gcd(batch, nb))
    spec = pl.BlockSpec((nb, MEM_LEN, d), lambda b: (b, 0, 0))
    return pl.pallas_call(
        _mem_kv_body,
        grid=(batch // nb,),
        in_specs=[spec, _const_spec((1, d)), wkv[0], _const_spec((1, XATTN_HEAD_DIM))],
        out_specs=[spec, spec],
        out_shape=[jax.ShapeDtypeStruct((batch, MEM_LEN, d), BF16)] * 2,
        compiler_params=_cparams(("parallel",)),
        name="mem_kv",
    )(mem, g.reshape(1, d), wkv[1], kn_g.reshape(1, XATTN_HEAD_DIM))


def _xattn_body(x_ref, m1_ref, m2_ref, wm_ref, gx_ref, wq_ref, qg_ref, k_ref, v_ref,
                wo_ref, o_ref):
    n1 = m1_ref.shape[1]
    x1 = (x_ref[...] + _dot(m1_ref[...], wm_ref[0:n1, :])
          + _dot(m2_ref[...], wm_ref[n1:, :]))
    q = _dot(_rms(x1, gx_ref[...]).astype(BF16), wq_ref[...])
    lanes = [slice(hh * XATTN_HEAD_DIM, (hh + 1) * XATTN_HEAD_DIM) for hh in range(XATTN_HEADS)]
    scores = [_dot_nt(_rms(q[:, ls], qg_ref[...]).astype(BF16), k_ref[:, ls])
              for ls in lanes]
    probs = []
    for s in scores:
        p = jnp.exp(s - jnp.max(s, axis=-1, keepdims=True))
        probs.append((p.astype(BF16), 1.0 / jnp.sum(p, axis=-1, keepdims=True)))
    heads = [(_dot(p, v_ref[:, ls]) * inv).astype(BF16) for (p, inv), ls in zip(probs, lanes)]
    o_ref[...] = x1 + _dot(jnp.concatenate(heads, axis=1), wo_ref[...])


def _xattn(x2d, m1, m2, w_mix, gx, wq, qn_g, k_mem, v_mem, wo, seq, tm=ROW_TILE):
    m, d = x2d.shape
    per_b = seq // tm
    row = lambda n: pl.BlockSpec((tm, n), lambda i: (i, 0))
    mem = pl.BlockSpec((None, MEM_LEN, d), lambda i: (i // per_b, 0, 0))
    qg = (qn_g * np.float32(XATTN_HEAD_DIM ** -0.5)).reshape(1, XATTN_HEAD_DIM)
    return pl.pallas_call(
        _xattn_body,
        grid=(m // tm,),
        in_specs=[row(d), row(m1.shape[1]), row(m2.shape[1]),
                  w_mix[0], _const_spec((1, d)),
                  wq[0], _const_spec((1, XATTN_HEAD_DIM)), mem, mem, wo[0]],
        out_specs=row(d),
        out_shape=jax.ShapeDtypeStruct((m, d), F32),
        compiler_params=_cparams(("parallel",)),
        name="xattn",
    )(x2d, m1, m2, w_mix[1], gx.reshape(1, d), wq[1], qg, k_mem, v_mem, wo[1])


def _mlp_body(x_ref, g_ref, w1_ref, w2_ref, o_ref, *, ff_tile):
    x = x_ref[...]
    h = _rms(x, g_ref[...]).astype(BF16)
    acc = x
    for c0 in range(0, D_FF, ff_tile):
        hid = jnp.maximum(_dot(h, w1_ref[:, c0:c0 + ff_tile]), 0.0)
        acc = acc + _dot((hid * hid).astype(BF16), w2_ref[c0:c0 + ff_tile, :])
    o_ref[...] = acc


def _mlp(x2d, g, w1, w2, tm=ROW_TILE, ff_tile=MLP_FF_TILE):
    m, d = x2d.shape
    row = pl.BlockSpec((tm, d), lambda i: (i, 0))
    return pl.pallas_call(
        functools.partial(_mlp_body, ff_tile=ff_tile),
        grid=(m // tm,),
        in_specs=[row, _const_spec((1, d)), w1[0], w2[0]],
        out_specs=row,
        out_shape=jax.ShapeDtypeStruct((m, d), F32),
        compiler_params=_cparams(("parallel",)),
        name="mlp",
    )(x2d, g.reshape(1, d), w1[1], w2[1])


def kernel(x, mem, norm_mix_g, ev_w_in, conv_dw_w, conv_dw_b, conv_ln_g, conv_ln_b,
           hgrn_lb_logits, hgrn_onorm_g, ev_w_out, od_w_in, sgu_ln_g, sgu_ln_b, sgu_w, sgu_b,
           attn_qnorm_g, attn_knorm_g, od_w_out, norm_xattn_g, norm_mem_g, xattn_wq, xattn_wkv,
           xattn_qnorm_g, xattn_knorm_g, xattn_wo, norm_mlp_g, mlp_w1, mlp_w2):
    batch, seq, d = x.shape
    (ev_w_in, ev_w_out, od_w_in, od_w_out, xattn_wq, xattn_wkv, xattn_wo, mlp_w1, mlp_w2) = (
        a.astype(BF16) for a in (ev_w_in, ev_w_out, od_w_in, od_w_out, xattn_wq, xattn_wkv,
                                 xattn_wo, mlp_w1, mlp_w2))
    lb_all = jnp.cumsum(jax.nn.softmax(hgrn_lb_logits.astype(F32), axis=0), axis=0)
    lb_all = lb_all - lb_all[0]
    xf = x.reshape(batch * seq, d)
    for l in range(DEPTH):
        if l % 2 == 0:
            e = l // 2
            m1, qig, f_pre = _even_proj(xf, norm_mix_g[l], _layer_weight(ev_w_in, e),
                                        conv_dw_w[e], conv_dw_b[e], conv_ln_g[e], conv_ln_b[e],
                                        batch, seq)
            m2 = _hgrn2(qig, f_pre, lb_all[e], hgrn_onorm_g[e], batch, seq)
            w_out = _layer_weight(ev_w_out, e)
        else:
            o = l // 2
            (p,) = _norm_proj(xf, norm_mix_g[l], _layer_weight(od_w_in, o), ODD_OUTS, ODD_PIECES)
            m1 = _chunked_sgu(p, sgu_ln_g[o], sgu_ln_b[o], sgu_w[o], sgu_b[o])
            m2 = _dilated_attention(p, attn_qnorm_g[o], attn_knorm_g[o], batch, seq)
            w_out = _layer_weight(od_w_out, o)
        k_mem, v_mem = _mem_kv(mem, norm_mem_g[l], _layer_weight(xattn_wkv, l),
                               xattn_knorm_g[l], batch)
        xf = _xattn(xf, m1, m2, w_out, norm_xattn_g[l], _layer_weight(xattn_wq, l),
                    xattn_qnorm_g[l], k_mem, v_mem, _layer_weight(xattn_wo, l), seq)
        xf = _mlp(xf, norm_mlp_g[l], _layer_weight(mlp_w1, l), _layer_weight(mlp_w2, l))
    return xf.reshape(batch, seq, d)
```

```python
import functools

import numpy as np
import jax
import jax.numpy as jnp
from jax import lax
from jax.experimental import pallas as pl
from jax.experimental.pallas import tpu as pltpu

F32 = jnp.float32
BF16 = jnp.bfloat16

D_MODEL = 1024
DEPTH = 4
EPS = 1e-6
CONV_DIM = 512
CONV_WIDTH = 31
HGRN_DIM = 512
HGRN_HEAD_DIM = 128
HGRN_HEADS = 4
HGRN_CHUNK = 64
SGU_DIM = 512
SGU_GROUPS = 4
SGU_CHUNK = 128
ATTN_HEAD_DIM = 64
DIL_CONFIGS = ((128, 1), (512, 4), (2048, 16))
ATTN_HEADS_PER_CFG = 4
ATTN_CFG_DIM = ATTN_HEADS_PER_CFG * ATTN_HEAD_DIM
ATTN_DIM = ATTN_CFG_DIM * len(DIL_CONFIGS)
ATTN_BLOCK = 128
ATTN_SPAN = 128
ROPE_THETA = 500000.0
ROT_DIM = 16
MEM_LEN = 256
XATTN_HEADS = 4
XATTN_HEAD_DIM = 256
D_FF = 4096

LANES = 128
SUBLANES = 8
ROW_TILE = 1024
HGRN_ROW_TILE = 2048
SGU_ROW_TILE = 2048
MEM_BATCH_TILE = 4
MLP_FF_TILE = 1024
CONV_HALO = 32
ATTN_SUPER = 1024
ATTN_PREP_ROWS = 256
ATTN_GROUP = 8
NEG_BIG = -1e30
VMEM_LIMIT = 56 * 1024 * 1024


def _cparams(sem):
    return pltpu.CompilerParams(dimension_semantics=sem, vmem_limit_bytes=VMEM_LIMIT)


def _const_spec(shape):
    nd = len(shape)
    return pl.BlockSpec(shape, lambda *_: (0,) * nd, pipeline_mode=pl.Buffered(1))


def _layer_weight(stack, layer):
    shape = stack.shape[1:]
    index = (layer,) + (0,) * len(shape)
    spec = pl.BlockSpec((None,) + shape, lambda *_: index, pipeline_mode=pl.Buffered(1))
    return spec, stack


def _dot(a, b):
    return jnp.dot(a, b, preferred_element_type=F32)


def _dot_nt(a, b):
    return lax.dot_general(a, b, (((1,), (1,)), ((), ())), preferred_element_type=F32)


def _dot_tn(a, b):
    return lax.dot_general(a, b, (((0,), (0,)), ((), ())), preferred_element_type=F32)


def _rms(x, g):
    return x * lax.rsqrt(jnp.mean(x * x, axis=-1, keepdims=True) + EPS) * g


def _layernorm(x, g, b):
    mu = jnp.mean(x, axis=-1, keepdims=True)
    xc = x - mu
    return xc * lax.rsqrt(jnp.mean(xc * xc, axis=-1, keepdims=True) + EPS) * g + b


def _sigmoid(x):
    return 1.0 / (1.0 + jnp.exp(-x))


def _norm_proj_body(x_ref, g_ref, w_ref, *o_refs, pieces):
    h = _rms(x_ref[...], g_ref[...]).astype(BF16)
    for out, w0, o0, n in pieces:
        o_refs[out][:, o0:o0 + n] = _dot(h, w_ref[:, w0:w0 + n]).astype(o_refs[out].dtype)


def _norm_proj(x2d, g, w, outs, pieces, tm=ROW_TILE):
    m, d = x2d.shape
    return pl.pallas_call(
        functools.partial(_norm_proj_body, pieces=pieces),
        grid=(m // tm,),
        in_specs=[pl.BlockSpec((tm, d), lambda i: (i, 0)),
                  _const_spec((1, d)), w[0]],
        out_specs=[pl.BlockSpec((tm, n), lambda i: (i, 0)) for n, _ in outs],
        out_shape=[jax.ShapeDtypeStruct((m, n), dt) for n, dt in outs],
        compiler_params=_cparams(("parallel",)),
        name="norm_proj",
    )(x2d, g.reshape(1, d), w[1])


ODD_IN = 2 * SGU_DIM + 3 * ATTN_DIM
ODD_OUTS = ((ODD_IN, BF16),)
ODD_PIECES = ((0, 0, 0, 2 * SGU_DIM),) + tuple(
    (0, 2 * SGU_DIM + i * ATTN_DIM, 2 * SGU_DIM + i * ATTN_DIM, ATTN_DIM) for i in range(3))


CONV_ROWS = 64
EVEN_ROW_TILE = 1024


def _even_proj_body(x_ref, g_ref, w_ref, cw_ref, cb_ref, lg_ref, lb_ref,
                    a_ref, qig_ref, f_ref, hbuf, sh_ref, *, tm):
    c = CONV_DIM
    hd = HGRN_DIM
    n_rows = CONV_HALO + tm
    h = _rms(x_ref[...], g_ref[...]).astype(BF16)
    a_in = _dot(h, w_ref[:, 0:2 * c])
    c0 = 2 * c
    qig_ref[:, 0:hd] = _dot(h, w_ref[:, c0:c0 + hd]).astype(BF16)
    f_ref[...] = _dot(h, w_ref[:, c0 + hd:c0 + 2 * hd])
    qig_ref[:, hd:3 * hd] = _dot(h, w_ref[:, c0 + 2 * hd:c0 + 4 * hd]).astype(BF16)

    @pl.when(pl.program_id(1) == 0)
    def _():
        hbuf[0:CONV_HALO, :] = jnp.zeros((CONV_HALO, c), F32)

    hbuf[CONV_HALO:n_rows, :] = a_in[:, :c] * _sigmoid(a_in[:, c:])
    for s in range(1, SUBLANES):
        sh_ref[s, 0:n_rows - SUBLANES, :] = hbuf[s:s + n_rows - SUBLANES, :]
    base = CONV_HALO - (CONV_WIDTH - 1)
    for r0 in range(0, tm, CONV_ROWS):
        acc = jnp.zeros((CONV_ROWS, c), F32) + cb_ref[...]
        for j in range(CONV_WIDTH):
            s = (base + j) % SUBLANES
            a0 = base + j - s + r0
            tap = hbuf[a0:a0 + CONV_ROWS, :] if s == 0 else sh_ref[s, a0:a0 + CONV_ROWS, :]
            acc = acc + cw_ref[j:j + 1, :] * tap
        y = _layernorm(acc, lg_ref[...], lb_ref[...])
        a_ref[r0:r0 + CONV_ROWS, :] = (y * _sigmoid(y)).astype(a_ref.dtype)
    hbuf[0:CONV_HALO, :] = hbuf[tm:n_rows, :]


def _even_proj(x2d, g, w, dw_w, dw_b, ln_g, ln_b, batch, seq, tm=EVEN_ROW_TILE):
    m, d = x2d.shape
    c = CONV_DIM
    hd = HGRN_DIM
    w_pad = jnp.zeros((CONV_HALO, c), F32).at[:CONV_WIDTH].set(dw_w)
    per_b = seq // tm
    row = lambda n: pl.BlockSpec((tm, n), lambda b, t: (b * per_b + t, 0))
    return pl.pallas_call(
        functools.partial(_even_proj_body, tm=tm),
        grid=(batch, per_b),
        in_specs=[row(d), _const_spec((1, d)), w[0],
                  _const_spec((CONV_HALO, c)), _const_spec((1, c)),
                  _const_spec((1, c)), _const_spec((1, c))],
        out_specs=[row(c), row(3 * hd), row(hd)],
        out_shape=[jax.ShapeDtypeStruct((m, c), BF16), jax.ShapeDtypeStruct((m, 3 * hd), BF16),
                   jax.ShapeDtypeStruct((m, hd), F32)],
        scratch_shapes=[pltpu.VMEM((CONV_HALO + tm, c), F32),
                        pltpu.VMEM((SUBLANES, CONV_HALO + tm, c), F32)],
        compiler_params=_cparams(("parallel", "arbitrary")),
        name="even_proj",
    )(x2d, g.reshape(1, d), w[1], w_pad, dw_b.reshape(1, c), ln_g.reshape(1, c),
      ln_b.reshape(1, c))


_HGRN_LEVELS = (32, 16, 8, 4, 2, 1)
_HGRN_MATRIX_LEVELS = (4, 2)
HGRN_GROUP = 4


def _hgrn_tables():
    n = HGRN_CHUNK
    t = np.arange(n)[:, None]
    u = np.arange(n)[None, :]
    blocks = [u <= t, u > t]
    masks = []
    offsets = {}
    row = 2 * n
    for m in _HGRN_LEVELS:
        c = (t // (2 * m)) * (2 * m)
        upper = (t - c) >= m
        if m in _HGRN_MATRIX_LEVELS:
            blocks.append(upper & (u >= c + m) & (u <= t))
            blocks.append((~upper) & (u >= t + 1) & (u <= c + m - 1))
            offsets[m] = (row, row + n)
            row += 2 * n
        cs = (u // (2 * m)) * (2 * m)
        masks.append(((t // (2 * m)) == (u // (2 * m))) & upper & ((u - cs) < m))
    d = np.concatenate(blocks, axis=0).astype(np.float32)
    return np.concatenate([d, d, d], axis=1), np.stack(masks).astype(np.float32), offsets


_HGRN_OFFSETS = _hgrn_tables()[2]


def _hgrn_body(q_ref, f_ref, i_ref, g_ref, lb_ref, og_ref, d_ref, m_ref, o_ref, st_ref,
               *, n_chunks):
    n = HGRN_CHUNK
    hd = HGRN_HEAD_DIM

    @pl.when(pl.program_id(1) == 0)
    def _():
        st_ref[...] = jnp.zeros(st_ref.shape, F32)

    lb = lb_ref[...]
    half = n // 2

    def level_operands(qh, kh, eh, li, m):
        if m == 1:
            return (qh * (1.0 - kh)).astype(BF16), kh.astype(BF16)
        if m in _HGRN_MATRIX_LEVELS:
            q0, k0 = _HGRN_OFFSETS[m]
            return ((qh * jnp.exp2(eh[q0:q0 + n])).astype(BF16),
                    (kh * jnp.exp2(eh[k0:k0 + n])).astype(BF16))
        e_b = eh[0:n]
        q_parts, k_parts = [], []
        for b in range(half // m):
            lo = slice(2 * b * m, (2 * b + 1) * m)
            up = slice((2 * b + 1) * m, (2 * b + 2) * m)
            ref = e_b[(2 * b + 1) * m - 1:(2 * b + 1) * m, :]
            q_parts += [qh[lo], qh[up] * jnp.exp2(e_b[up] - ref)]
            k_parts += [kh[lo] * jnp.exp2(ref - e_b[lo]), kh[up]]
        return (jnp.concatenate(q_parts, axis=0).astype(BF16),
                jnp.concatenate(k_parts, axis=0).astype(BF16))

    def group(gi, carry):
        chunks = []
        for ci in range(HGRN_GROUP):
            rows = pl.ds(pl.multiple_of((gi * HGRN_GROUP + ci) * n, n), n)
            f = lb + (1.0 - lb) * _sigmoid(f_ref[rows, :])
            logf = jnp.log2(f)
            hi = logf.astype(BF16)
            r1 = logf - hi.astype(F32)
            mid = r1.astype(BF16)
            lo = (r1 - mid.astype(F32)).astype(BF16)
            expo = _dot(d_ref[...], jnp.concatenate([hi, mid, lo], axis=0))
            chunks.append((rows, 1.0 - f, expo, q_ref[rows, :].astype(F32), i_ref[rows, :],
                           g_ref[rows, :].astype(F32)))
        items = []
        for rows, kk, expo, q, v, g in chunks:
            for h in range(HGRN_HEADS):
                ls = slice(h * hd, (h + 1) * hd)
                items.append((rows, ls, h, q[:, ls], kk[:, ls], v[:, ls], g[:, ls], expo[:, ls]))
        operands = [[level_operands(qh, kh, eh, li, m) for li, m in enumerate(_HGRN_LEVELS)]
                    for _, _, _, qh, kh, _, _, eh in items]
        q_inter = [(qh * jnp.exp2(eh[0:n])).astype(BF16) for _, _, _, qh, _, _, _, eh in items]
        k_state = [(kh * jnp.exp2(eh[n:2 * n])).astype(BF16) for _, _, _, _, kh, _, _, eh in items]
        scores = []
        for ops in operands:
            sc = jnp.zeros((n, n), F32)
            for li, (qs, ks) in enumerate(ops):
                sc = sc + m_ref[li] * _dot_nt(qs, ks)
            scores.append(sc.astype(BF16))
        intra = [_dot(sc, it[5]) for sc, it in zip(scores, items)]
        kv = [_dot_tn(it[5], ks) for ks, it in zip(k_state, items)]
        inter = []
        for idx, (_, _, h, _, _, _, _, eh) in enumerate(items):
            st = st_ref[h]
            inter.append(_dot_nt(q_inter[idx], st.astype(BF16)))
            st_ref[h] = st * jnp.exp2(eh[n - 1:n, :]) + kv[idx]
        for idx, (rows, ls, _, qh, kh, vh, gh, _) in enumerate(items):
            o = inter[idx] + intra[idx]
            o = o + jnp.sum(qh * kh, axis=-1, keepdims=True) * vh.astype(F32)
            on = _rms(o, og_ref[...])
            o_ref[rows, ls] = (on * (gh * _sigmoid(gh))).astype(o_ref.dtype)
        return carry

    lax.fori_loop(0, n_chunks // HGRN_GROUP, group, 0)


def _hgrn2(qig, f_pre, lb, onorm_g, batch, seq, tt=HGRN_ROW_TILE):
    c = HGRN_DIM
    d3, masks, _ = _hgrn_tables()
    p3 = qig.reshape(batch, seq, qig.shape[1])
    f3 = f_pre.reshape(batch, seq, c)
    col = lambda j: pl.BlockSpec((None, tt, c), lambda b, t: (b, t, j))
    out = pl.pallas_call(
        functools.partial(_hgrn_body, n_chunks=tt // HGRN_CHUNK),
        grid=(batch, seq // tt),
        in_specs=[col(0), col(0), col(1), col(2),
                  _const_spec((1, c)), _const_spec((1, HGRN_HEAD_DIM)),
                  _const_spec(d3.shape), _const_spec(masks.shape)],
        out_specs=pl.BlockSpec((None, tt, c), lambda b, t: (b, t, 0)),
        out_shape=jax.ShapeDtypeStruct((batch, seq, c), BF16),
        scratch_shapes=[pltpu.VMEM((HGRN_HEADS, HGRN_HEAD_DIM, HGRN_HEAD_DIM), F32)],
        compiler_params=_cparams(("parallel", "arbitrary")),
        name="hgrn2",
    )(p3, f3, p3, p3, lb.reshape(1, c), onorm_g.reshape(1, HGRN_HEAD_DIM),
      jnp.asarray(d3, BF16), jnp.asarray(masks, F32))
    return out.reshape(batch * seq, c)


def _sgu_body(z_ref, lg_ref, lb_ref, w_ref, bias_ref, o_ref, *, tt):
    c = SGU_DIM
    gw = c // SGU_GROUPS
    z = z_ref[...].astype(F32)
    z = 0.5 * z * (1.0 + lax.erf(z * np.float32(1.0 / np.sqrt(2.0))))
    u = z[:, :c]
    v = _layernorm(z[:, c:], lg_ref[...], lb_ref[...]).astype(BF16)
    for ci in range(tt // SGU_CHUNK):
        rs = slice(ci * SGU_CHUNK, (ci + 1) * SGU_CHUNK)
        for gi in range(SGU_GROUPS):
            ls = slice(gi * gw, (gi + 1) * gw)
            mixed = _dot(w_ref[gi], v[rs, ls]) + bias_ref[:, ls]
            o_ref[rs, ls] = (u[rs, ls] * mixed).astype(o_ref.dtype)


def _chunked_sgu(p, ln_g, ln_b, w_s, b_s, tt=SGU_ROW_TILE):
    m = p.shape[0]
    c = SGU_DIM
    tril = np.tril(np.ones((SGU_CHUNK, SGU_CHUNK), dtype=bool))
    w = jnp.where(tril[None], w_s, 0.0).astype(BF16)
    bias = jnp.repeat(b_s.T, c // SGU_GROUPS, axis=1)
    return pl.pallas_call(
        functools.partial(_sgu_body, tt=tt),
        grid=(m // tt,),
        in_specs=[pl.BlockSpec((tt, 2 * c), lambda i: (i, 0)),
                  _const_spec((1, c)), _const_spec((1, c)),
                  _const_spec(w.shape), _const_spec((SGU_CHUNK, c))],
        out_specs=pl.BlockSpec((tt, c), lambda i: (i, 0)),
        out_shape=jax.ShapeDtypeStruct((m, c), BF16),
        compiler_params=_cparams(("parallel",)),
        name="chunked_sgu",
    )(p, ln_g.reshape(1, c), ln_b.reshape(1, c), w, bias)


def _rope_tables(seq):
    half = ROT_DIM // 2
    inv_freq = jnp.power(ROPE_THETA, -jnp.arange(half, dtype=F32) / half)
    ang = jnp.arange(seq, dtype=F32)[:, None] * inv_freq[None, :]
    cos, sin = jnp.cos(ang), jnp.sin(ang)
    ones = jnp.ones((seq, ATTN_HEAD_DIM - ROT_DIM), F32)
    zeros = jnp.zeros((seq, ATTN_HEAD_DIM - ROT_DIM), F32)
    zh = jnp.zeros((seq, half), F32)
    c_head = jnp.concatenate([cos, cos, ones], axis=1)
    s_dn_head = jnp.concatenate([zh, sin, zeros], axis=1)
    s_up_head = jnp.concatenate([-sin, zh, zeros], axis=1)
    tile = lambda a: jnp.concatenate([a, a], axis=1)
    return tile(c_head), tile(s_dn_head), tile(s_up_head)


def _attn_geometry(dilation):
    per_class = ATTN_SUPER // dilation
    rows = min(per_class, ATTN_BLOCK)
    return per_class, rows, per_class // rows


def _attn_prep(x_ref, g_ref, c_ref, sd_ref, su_ref, bd_ref, dst_ref):
    half = ROT_DIM // 2
    for pair in range(ATTN_CFG_DIM // LANES):
        ls = slice(pair * LANES, (pair + 1) * LANES)
        for r0 in range(0, ATTN_SUPER, ATTN_PREP_ROWS):
            rr = slice(r0, r0 + ATTN_PREP_ROWS)
            x = x_ref[rr, ls].astype(F32)
            ms = _dot((x * x).astype(BF16), bd_ref[...])
            y = x * lax.rsqrt(ms + EPS) * g_ref[:, ls]
            dst_ref[pair, rr, :] = (y * c_ref[rr, :] + pltpu.roll(y, half, 1) * sd_ref[rr, :]
                                    + pltpu.roll(y, LANES - half, 1) * su_ref[rr, :])


def _attn_band(rows):
    blk = ATTN_BLOCK
    qi = lax.broadcasted_iota(jnp.int32, (2 * rows, blk + rows), 0) % rows
    key_col = lax.broadcasted_iota(jnp.int32, (2 * rows, blk + rows), 1)
    dist = qi - (key_col - blk)
    band_bias = jnp.where((dist >= 0) & (dist <= ATTN_SPAN), 0.0, NEG_BIG).astype(F32)
    return band_bias, key_col


def _attn_units(dilation, units, step, band_bias, key_col, qd_ref, kd_ref, vd_ref, res_refs):
    per_class, rows, _ = _attn_geometry(dilation)
    blk = ATTN_BLOCK
    od_ref, md_ref, dd_ref = res_refs
    head0 = lax.broadcasted_iota(jnp.int32, (1, LANES), 1) < ATTN_HEAD_DIM
    items = []
    for r, j in units:
        q_rows = pl.ds(pl.multiple_of(r * per_class + j * blk, rows), rows)
        k_rows = pl.ds(pl.multiple_of(j * blk, blk), blk + rows)
        first_key = jnp.maximum(blk - (step * per_class + j * blk), 0)
        bias = jnp.where(key_col >= first_key, band_bias, NEG_BIG)
        for pair in range(ATTN_CFG_DIM // LANES):
            items.append((pair, r, q_rows, k_rows, bias))
    scores = []
    for pair, r, q_rows, k_rows, bias in items:
        qp = qd_ref[pair, q_rows, :]
        zero = jnp.zeros_like(qp)
        q2 = jnp.concatenate([jnp.where(head0, qp, zero), jnp.where(head0, zero, qp)], axis=0)
        scores.append(_dot_nt(q2, kd_ref[pair, r, k_rows, :]) + bias)
    probs = []
    for s in scores:
        mx = jnp.max(s, axis=-1, keepdims=True)
        p = jnp.exp2(s - mx)
        probs.append((p.astype(BF16), mx, jnp.sum(p, axis=-1, keepdims=True)))
    outs = [_dot(p, vd_ref[pair, r, k_rows, :])
            for (p, _, _), (pair, r, _, k_rows, _) in zip(probs, items)]
    for o2, (_, mx, den), (pair, _, q_rows, _, _) in zip(outs, probs, items):
        od_ref[pair, q_rows, :] = jnp.where(head0, o2[:rows], o2[rows:])
        md_ref[pair, q_rows, :] = jnp.where(head0, mx[:rows], mx[rows:])
        dd_ref[pair, q_rows, :] = jnp.where(head0, den[:rows], den[rows:])


def _attn_config(cfg, dilation, step, ks_ref, v_ref, vs_ref, qs_ref, qd_ref, res_refs,
                 kd_ref, vd_ref, acc_refs):
    per_class, rows, n_blk = _attn_geometry(dilation)
    blk = ATTN_BLOCK
    pairs = ATTN_CFG_DIM // LANES

    @pl.when(step == 0)
    def _():
        zeros = jnp.zeros((pairs, dilation, blk, LANES), BF16)
        kd_ref[:, :, 0:blk, :] = zeros
        vd_ref[:, :, 0:blk, :] = zeros

    cur = slice(blk, blk + per_class)
    for pair in range(pairs):
        ls = slice(pair * LANES, (pair + 1) * LANES)
        if dilation == 1:
            qd_ref[pair] = qs_ref[pair].astype(BF16)
            kd_ref[pair, 0, cur, :] = ks_ref[pair].astype(BF16)
            vd_ref[pair, 0, cur, :] = v_ref[:, ls]
        else:
            vs_ref[pair] = v_ref[:, ls].astype(F32)
            for r in range(dilation):
                src = pl.ds(r, per_class, stride=dilation)
                qd_ref[pair, r * per_class:(r + 1) * per_class, :] = (
                    qs_ref[pair, src, :].astype(BF16))
                kd_ref[pair, r, cur, :] = ks_ref[pair, src, :].astype(BF16)
                vd_ref[pair, r, cur, :] = vs_ref[pair, src, :].astype(BF16)

    band_bias, key_col = _attn_band(rows)

    def group(gi, carry):
        units = [((gi * ATTN_GROUP + i) % dilation, (gi * ATTN_GROUP + i) // dilation)
                 for i in range(ATTN_GROUP)]
        _attn_units(dilation, units, step, band_bias, key_col, qd_ref, kd_ref, vd_ref, res_refs)
        return carry

    lax.fori_loop(0, dilation * n_blk // ATTN_GROUP, group, 0)

    for pair in range(pairs):
        for r in range(dilation):
            src = slice(r * per_class, (r + 1) * per_class)
            dst = pl.ds(r, per_class, stride=dilation) if dilation > 1 else slice(0, per_class)
            for acc_ref, res_ref in zip(acc_refs, res_refs):
                acc_ref[cfg, pair, dst, :] = res_ref[pair, src, :]

    for ref in (kd_ref, vd_ref):
        ref[:, :, 0:blk, :] = ref[:, :, per_class:per_class + blk, :]


def _dil_attn_body(q_ref, k_ref, v_ref, qg_ref, kg_ref, c_ref, sd_ref, su_ref, bd_ref,
                   o_ref, qs_ref, ks_ref, vs_ref, od_ref, md_ref, dd_ref, qd_ref,
                   kd0, kd1, kd2, vd0, vd1, vd2, oacc_ref, macc_ref, dacc_ref):
    step = pl.program_id(1)
    cfg_id = pl.program_id(2)
    n_cfg = len(DIL_CONFIGS)
    _attn_prep(q_ref, qg_ref, c_ref, sd_ref, su_ref, bd_ref, qs_ref)
    _attn_prep(k_ref, kg_ref, c_ref, sd_ref, su_ref, bd_ref, ks_ref)
    kds, vds = (kd0, kd1, kd2), (vd0, vd1, vd2)
    for cfg, (_, dilation) in enumerate(DIL_CONFIGS):
        @pl.when(cfg_id == cfg)
        def _(cfg=cfg, dilation=dilation):
            _attn_config(cfg, dilation, step, ks_ref, v_ref, vs_ref, qs_ref, qd_ref,
                         (od_ref, md_ref, dd_ref), kds[cfg], vds[cfg],
                         (oacc_ref, macc_ref, dacc_ref))

    @pl.when(cfg_id == n_cfg - 1)
    def _():
        for pair in range(ATTN_CFG_DIM // LANES):
            for r0 in range(0, ATTN_SUPER, ATTN_PREP_ROWS):
                rr = slice(r0, r0 + ATTN_PREP_ROWS)
                ms = [macc_ref[c, pair, rr, :] for c in range(n_cfg)]
                top = jnp.maximum(jnp.maximum(ms[0], ms[1]), ms[2])
                ws = [jnp.exp2(mc - top) for mc in ms]
                total = sum(dacc_ref[c, pair, rr, :] * ws[c] for c in range(n_cfg))
                inv = 1.0 / total
                for c in range(n_cfg):
                    col = c * ATTN_CFG_DIM + pair * LANES
                    o_ref[rr, col:col + LANES] = (
                        oacc_ref[c, pair, rr, :] * (ws[c] * inv)).astype(o_ref.dtype)


def _dilated_attention(p, qn_g, kn_g, batch, seq):
    m = p.shape[0]
    n_cfg = len(DIL_CONFIGS)
    n_sb = seq // ATTN_SUPER
    pairs = ATTN_CFG_DIM // LANES
    c_t, sd_t, su_t = _rope_tables(seq)
    head = np.arange(LANES) // ATTN_HEAD_DIM
    bd = jnp.asarray((head[:, None] == head[None, :]).astype(np.float32) / ATTN_HEAD_DIM, BF16)
    tile_g = lambda g: jnp.tile(g, ATTN_CFG_DIM // ATTN_HEAD_DIM).reshape(1, ATTN_CFG_DIM)
    qg = tile_g(qn_g) * np.float32(ATTN_HEAD_DIM ** -0.5 * np.log2(np.e))
    q0 = 2 * SGU_DIM // ATTN_CFG_DIM
    k0 = q0 + n_cfg
    v0 = k0 + n_cfg
    wide = lambda c0: pl.BlockSpec((ATTN_SUPER, ATTN_CFG_DIM),
                                   lambda b, s, c: (b * n_sb + s, c0 + c))
    tab = pl.BlockSpec((ATTN_SUPER, LANES), lambda b, s, c: (s, 0))
    hist = lambda d: pltpu.VMEM((pairs, d, ATTN_BLOCK + ATTN_SUPER // d, LANES), BF16)
    dils = [d for _, d in DIL_CONFIGS]
    return pl.pallas_call(
        _dil_attn_body,
        grid=(batch, n_sb, n_cfg),
        in_specs=[wide(q0), wide(k0), wide(v0),
                  _const_spec((1, ATTN_CFG_DIM)), _const_spec((1, ATTN_CFG_DIM)),
                  tab, tab, tab, _const_spec((LANES, LANES))],
        out_specs=pl.BlockSpec((ATTN_SUPER, ATTN_DIM), lambda b, s, c: (b * n_sb + s, 0)),
        out_shape=jax.ShapeDtypeStruct((m, ATTN_DIM), BF16),
        scratch_shapes=[pltpu.VMEM((pairs, ATTN_SUPER, LANES), F32)] * 6
        + [pltpu.VMEM((pairs, ATTN_SUPER, LANES), BF16)]
        + [hist(d) for d in dils] + [hist(d) for d in dils]
        + [pltpu.VMEM((n_cfg, pairs, ATTN_SUPER, LANES), F32)] * 3,
        compiler_params=_cparams(("parallel", "arbitrary", "arbitrary")),
        name="dil_attn",
    )(p, p, p, qg, tile_g(kn_g), c_t, sd_t, su_t, bd)


def _mem_kv_body(mem_ref, g_ref, w_ref, kg_ref, k_ref, v_ref):
    d = D_MODEL
    nb = mem_ref.shape[0]
    h = _rms(mem_ref[...].reshape(nb * MEM_LEN, d), g_ref[...]).astype(BF16)
    for hh in range(XATTN_HEADS):
        ls = slice(hh * XATTN_HEAD_DIM, (hh + 1) * XATTN_HEAD_DIM)
        kh = _rms(_dot(h, w_ref[:, ls]), kg_ref[...])
        k_ref[:, :, ls] = kh.reshape(nb, MEM_LEN, XATTN_HEAD_DIM).astype(k_ref.dtype)
    v_ref[...] = _dot(h, w_ref[:, d:]).reshape(nb, MEM_LEN, d).astype(v_ref.dtype)


def _mem_kv(mem, g, wkv, kn_g, batch, nb=MEM_BATCH_TILE):
    d = D_MODEL
    nb = int(np.gcd(batch, nb))
    spec = pl.BlockSpec((nb, MEM_LEN, d), lambda b: (b, 0, 0))
    return pl.pallas_call(
        _mem_kv_body,
        grid=(batch // nb,),
        in_specs=[spec, _const_spec((1, d)), wkv[0], _const_spec((1, XATTN_HEAD_DIM))],
        out_specs=[spec, spec],
        out_shape=[jax.ShapeDtypeStruct((batch, MEM_LEN, d), BF16)] * 2,
        compiler_params=_cparams(("parallel",)),
        name="mem_kv",
    )(mem, g.reshape(1, d), wkv[1], kn_g.reshape(1, XATTN_HEAD_DIM))


def _xattn_body(x_ref, m1_ref, m2_ref, wm_ref, gx_ref, wq_ref, qg_ref, k_ref, v_ref,
                wo_ref, o_ref):
    mix = jnp.concatenate([m1_ref[...], m2_ref[...]], axis=1)
    x1 = x_ref[...] + _dot(mix, wm_ref[...])
    q = _dot(_rms(x1, gx_ref[...]).astype(BF16), wq_ref[...])
    lanes = [slice(hh * XATTN_HEAD_DIM, (hh + 1) * XATTN_HEAD_DIM) for hh in range(XATTN_HEADS)]
    scores = [_dot_nt(_rms(q[:, ls], qg_ref[...]).astype(BF16), k_ref[:, ls])
              for ls in lanes]
    probs = []
    for s in scores:
        p = jnp.exp(s - jnp.max(s, axis=-1, keepdims=True))
        probs.append((p.astype(BF16), 1.0 / jnp.sum(p, axis=-1, keepdims=True)))
    heads = [(_dot(p, v_ref[:, ls]) * inv).astype(BF16) for (p, inv), ls in zip(probs, lanes)]
    o_ref[...] = x1 + _dot(jnp.concatenate(heads, axis=1), wo_ref[...])


def _xattn(x2d, m1, m2, w_mix, gx, wq, qn_g, k_mem, v_mem, wo, seq, tm=ROW_TILE):
    m, d = x2d.shape
    per_b = seq // tm
    row = lambda n: pl.BlockSpec((tm, n), lambda i: (i, 0))
    mem = pl.BlockSpec((None, MEM_LEN, d), lambda i: (i // per_b, 0, 0))
    qg = (qn_g * np.float32(XATTN_HEAD_DIM ** -0.5)).reshape(1, XATTN_HEAD_DIM)
    return pl.pallas_call(
        _xattn_body,
        grid=(m // tm,),
        in_specs=[row(d), row(m1.shape[1]), row(m2.shape[1]),
                  w_mix[0], _const_spec((1, d)),
                  wq[0], _const_spec((1, XATTN_HEAD_DIM)), mem, mem, wo[0]],
        out_specs=row(d),
        out_shape=jax.ShapeDtypeStruct((m, d), F32),
        compiler_params=_cparams(("parallel",)),
        name="xattn",
    )(x2d, m1, m2, w_mix[1], gx.reshape(1, d), wq[1], qg, k_mem, v_mem, wo[1])


def _mlp_body(x_ref, g_ref, w1_ref, w2_ref, o_ref, *, ff_tile):
    x = x_ref[...]
    h = _rms(x, g_ref[...]).astype(BF16)
    acc = x
    for c0 in range(0, D_FF, ff_tile):
        hid = jnp.maximum(_dot(h, w1_ref[:, c0:c0 + ff_tile]), 0.0)
        acc = acc + _dot((hid * hid).astype(BF16), w2_ref[c0:c0 + ff_tile, :])
    o_ref[...] = acc


def _mlp(x2d, g, w1, w2, tm=ROW_TILE, ff_tile=MLP_FF_TILE):
    m, d = x2d.shape
    row = pl.BlockSpec((tm, d), lambda i: (i, 0))
    return pl.pallas_call(
        functools.partial(_mlp_body, ff_tile=ff_tile),
        grid=(m // tm,),
        in_specs=[row, _const_spec((1, d)), w1[0], w2[0]],
        out_specs=row,
        out_shape=jax.ShapeDtypeStruct((m, d), F32),
        compiler_params=_cparams(("parallel",)),
        name="mlp",
    )(x2d, g.reshape(1, d), w1[1], w2[1])


def kernel(x, mem, norm_mix_g, ev_w_in, conv_dw_w, conv_dw_b, conv_ln_g, conv_ln_b,
           hgrn_lb_logits, hgrn_onorm_g, ev_w_out, od_w_in, sgu_ln_g, sgu_ln_b, sgu_w, sgu_b,
           attn_qnorm_g, attn_knorm_g, od_w_out, norm_xattn_g, norm_mem_g, xattn_wq, xattn_wkv,
           xattn_qnorm_g, xattn_knorm_g, xattn_wo, norm_mlp_g, mlp_w1, mlp_w2):
    batch, seq, d = x.shape
    (ev_w_in, ev_w_out, od_w_in, od_w_out, xattn_wq, xattn_wkv, xattn_wo, mlp_w1, mlp_w2) = (
        a.astype(BF16) for a in (ev_w_in, ev_w_out, od_w_in, od_w_out, xattn_wq, xattn_wkv,
                                 xattn_wo, mlp_w1, mlp_w2))
    lb_all = jnp.cumsum(jax.nn.softmax(hgrn_lb_logits.astype(F32), axis=0), axis=0)
    lb_all = lb_all - lb_all[0]
    xf = x.reshape(batch * seq, d)
    for l in range(DEPTH):
        if l % 2 == 0:
            e = l // 2
            m1, qig, f_pre = _even_proj(xf, norm_mix_g[l], _layer_weight(ev_w_in, e),
                                        conv_dw_w[e], conv_dw_b[e], conv_ln_g[e], conv_ln_b[e],
                                        batch, seq)
            m2 = _hgrn2(qig, f_pre, lb_all[e], hgrn_onorm_g[e], batch, seq)
            w_out = _layer_weight(ev_w_out, e)
        else:
            o = l // 2
            (p,) = _norm_proj(xf, norm_mix_g[l], _layer_weight(od_w_in, o), ODD_OUTS, ODD_PIECES)
            m1 = _chunked_sgu(p, sgu_ln_g[o], sgu_ln_b[o], sgu_w[o], sgu_b[o])
            m2 = _dilated_attention(p, attn_qnorm_g[o], attn_knorm_g[o], batch, seq)
            w_out = _layer_weight(od_w_out, o)
        k_mem, v_mem = _mem_kv(mem, norm_mem_g[l], _layer_weight(xattn_wkv, l),
                               xattn_knorm_g[l], batch)
        xf = _xattn(xf, m1, m2, w_out, norm_xattn_g[l], _layer_weight(xattn_wq, l),
                    xattn_qnorm_g[l], k_mem, v_mem, _layer_weight(xattn_wo, l), seq)
        xf = _mlp(xf, norm_mlp_g[l], _layer_weight(mlp_w1, l), _layer_weight(mlp_w2, l))
    return xf.reshape(batch, seq, d)
```
